```python
import jax, jax.numpy as jnp
from jax import lax
import numpy as np

D_MODEL = 4096
BATCH = 4
SEQ = 2048
DEPTH = 1
DEC_BATCH = 128
DEC_SEQ = 4
PAST_LEN = 2048
PAGE_SIZE = 128

D_MIX = D_MODEL
D_ATT = D_MIX // 2
HEAD_DIM = 128
N_ATT_HEADS = D_ATT // HEAD_DIM
D_CONV = D_MIX - D_ATT
CONV_WIDTH = 3
D_FF = ((8 * D_MODEL // 3 + 255) // 256) * 256
Q_BLOCK = 128
EPS = 1e-6
SB_BIAS_INIT = -6.0

kernel_name = "hymba_stickbreak_shortconv_convffn_step"


def rmsnorm(x, g):
    xf = x.astype(jnp.float32)
    y = xf * lax.rsqrt(jnp.mean(xf * xf, axis=-1, keepdims=True) + EPS)
    return (y * g.astype(jnp.float32)).astype(x.dtype)


def adaln_params(c, w_ada, b_ada):
    mod = jax.nn.silu(c) @ w_ada + b_ada
    return jnp.split(mod[:, None, :], 6, axis=-1)


def causal_dwconv(u, prev, w):
    t = u.shape[1]
    u_pad = jnp.concatenate([prev.astype(u.dtype), u], axis=1)
    y = w[0] * u_pad[:, 0:t]
    for i in range(1, CONV_WIDTH):
        y = y + w[i] * u_pad[:, i:i + t]
    return y, u_pad[:, -(CONV_WIDTH - 1):]


def stick_breaking(q, k, v, bias, q_pos, k_pos):
    z = jnp.einsum("nqhd,nkhd->nhqk", q, k, preferred_element_type=jnp.float32) * (HEAD_DIM ** -0.5)
    z = z + bias.astype(jnp.float32)[None, :, None, None]
    causal = (k_pos[None, :] < q_pos[:, None])[None, None]
    log_keep = jnp.where(causal, jax.nn.log_sigmoid(-z), 0.0)
    log_after = lax.cumsum(log_keep, axis=3, reverse=True) - log_keep
    a = jnp.where(causal, jnp.exp(jax.nn.log_sigmoid(z) + log_after), 0.0)
    out = jnp.einsum("nhqk,nkhd->nqhd", a.astype(v.dtype), v, preferred_element_type=jnp.float32)
    return out.astype(v.dtype)


def prompt_attention(q, k, v, bias):
    n, s, h, dh = q.shape
    nqb = s // Q_BLOCK
    q_blocks = q.reshape(n, nqb, Q_BLOCK, h, dh).transpose(1, 0, 2, 3, 4)
    pos = jnp.arange(s, dtype=jnp.int32)
    pos_blocks = pos.reshape(nqb, Q_BLOCK)
    out = lax.map(lambda qp: stick_breaking(qp[0], k, v, bias, qp[1], pos), (q_blocks, pos_blocks))
    return out.transpose(1, 0, 2, 3, 4).reshape(n, s, h, dh)


def make_paged_attention(k_pages, v_pages, page_table):
    def attend(q, k_new, v_new, bias):
        db, t = q.shape[0], q.shape[1]
        past_len = page_table.shape[1] * k_pages.shape[1]
        k_past = k_pages[page_table].reshape(db, past_len, N_ATT_HEADS, HEAD_DIM)
        v_past = v_pages[page_table].reshape(db, past_len, N_ATT_HEADS, HEAD_DIM)
        k_all = jnp.concatenate([k_past.astype(k_new.dtype), k_new], axis=1)
        v_all = jnp.concatenate([v_past.astype(v_new.dtype), v_new], axis=1)
        q_pos = past_len + jnp.arange(t, dtype=jnp.int32)
        k_pos = jnp.arange(past_len + t, dtype=jnp.int32)
        return stick_breaking(q, k_all, v_all, bias, q_pos, k_pos)
    return attend


def decoder_layer(x, c, attend, conv_prev, ffn_prev, w_ada, b_ada, g_pre_mix, g_post_mix, w_in, sb_bias, conv_w,
                  g_attn_out, g_conv_out, w_o, g_pre_ffn, g_post_ffn, w_fc1, ffn_conv_w, w_fc2):
    n, t, _ = x.shape
    sh1, sc1, gt1, sh2, sc2, gt2 = adaln_params(c, w_ada, b_ada)
    h = rmsnorm(x, g_pre_mix) * (1 + sc1) + sh1
    proj = h @ w_in
    cuts = [D_ATT, 2 * D_ATT, 3 * D_ATT, 3 * D_ATT + D_CONV, 3 * D_ATT + 2 * D_CONV]
    q, k, v, b_gate, c_gate, u = jnp.split(proj, cuts, axis=-1)
    q = q.reshape(n, t, N_ATT_HEADS, HEAD_DIM)
    k = k.reshape(n, t, N_ATT_HEADS, HEAD_DIM)
    v = v.reshape(n, t, N_ATT_HEADS, HEAD_DIM)
    att = attend(q, k, v, sb_bias).reshape(n, t, D_ATT)
    conv_y, conv_state = causal_dwconv(c_gate * u, conv_prev, conv_w)
    short = b_gate * conv_y
    merged = jnp.concatenate([rmsnorm(att, g_attn_out), rmsnorm(short, g_conv_out)], axis=-1)
    x = x + gt1 * rmsnorm(merged @ w_o, g_post_mix)
    h = rmsnorm(x, g_pre_ffn) * (1 + sc2) + sh2
    a, b = jnp.split(h @ w_fc1, 2, axis=-1)
    a_conv, ffn_state = causal_dwconv(a, ffn_prev, ffn_conv_w)
    f = (jax.nn.silu(a_conv) * b) @ w_fc2
    x = x + gt2 * rmsnorm(f, g_post_ffn)
    return x, k, v, conv_state, ffn_state


def setup_inputs(seed: int = 0) -> dict:
    key = jax.random.key(seed)
    ks = jax.random.split(key, 24)
    n_pages = PAST_LEN // PAGE_SIZE
    n_used = DEC_BATCH * n_pages
    n_phys = n_used + max(1, n_used // 4)
    nrm = jax.random.normal
    f32 = jnp.float32

    def gain(k, dim):
        return 1.0 + 0.01 * nrm(k, (DEPTH, dim), f32)

    page_table = jax.random.permutation(ks[0], n_phys)[:n_used].reshape(DEC_BATCH, n_pages).astype(jnp.int32)
    return {
        "x_prompt": nrm(ks[1], (BATCH, SEQ, D_MODEL), f32),
        "x_sample": nrm(ks[2], (DEC_BATCH, DEC_SEQ, D_MODEL), f32),
        "c_prompt": nrm(ks[3], (BATCH, D_MODEL), f32),
        "c_sample": nrm(ks[4], (DEC_BATCH, D_MODEL), f32),
        "cache_k": nrm(ks[5], (DEPTH, n_phys, PAGE_SIZE, N_ATT_HEADS, HEAD_DIM), f32),
        "cache_v": nrm(ks[6], (DEPTH, n_phys, PAGE_SIZE, N_ATT_HEADS, HEAD_DIM), f32),
        "state_conv": nrm(ks[7], (DEPTH, DEC_BATCH, CONV_WIDTH - 1, D_CONV), f32),
        "state_ffn_conv": nrm(ks[8], (DEPTH, DEC_BATCH, CONV_WIDTH - 1, D_FF), f32),
        "page_table": page_table,
        "w_ada": nrm(ks[9], (DEPTH, D_MODEL, 6 * D_MODEL), f32) * (0.5 * D_MODEL ** -0.5),
        "b_ada": 0.01 * nrm(ks[10], (DEPTH, 6 * D_MODEL), f32),
        "g_pre_mix": gain(ks[11], D_MODEL),
        "g_post_mix": gain(ks[12], D_MODEL),
        "w_in": nrm(ks[13], (DEPTH, D_MODEL, 3 * D_ATT + 3 * D_CONV), f32) * (D_MODEL ** -0.5),
        "sb_bias": SB_BIAS_INIT + 0.1 * nrm(ks[23], (DEPTH, N_ATT_HEADS), f32),
        "conv_w": nrm(ks[14], (DEPTH, CONV_WIDTH, D_CONV), f32) * 0.5,
        "g_attn_out": gain(ks[15], D_ATT),
        "g_conv_out": gain(ks[16], D_CONV),
        "w_o": nrm(ks[17], (DEPTH, D_MIX, D_MODEL), f32) * (D_MIX ** -0.5),
        "g_pre_ffn": gain(ks[18], D_MODEL),
        "g_post_ffn": gain(ks[19], D_MODEL),
        "w_fc1": nrm(ks[20], (DEPTH, D_MODEL, 2 * D_FF), f32) * (D_MODEL ** -0.5),
        "ffn_conv_w": nrm(ks[21], (DEPTH, CONV_WIDTH, D_FF), f32) * 0.5,
        "w_fc2": nrm(ks[22], (DEPTH, D_FF, D_MODEL), f32) * (D_FF ** -0.5),
    }


def reference(x_prompt, x_sample, c_prompt, c_sample, cache_k, cache_v, state_conv, state_ffn_conv, page_table,
              w_ada, b_ada, g_pre_mix, g_post_mix, w_in, sb_bias, conv_w, g_attn_out, g_conv_out, w_o,
              g_pre_ffn, g_post_ffn, w_fc1, ffn_conv_w, w_fc2):
    xp, xs = x_prompt, x_sample
    bp = x_prompt.shape[0]
    kp_l, vp_l, cp_l, fp_l = [], [], [], []
    ks_l, vs_l, cs_l, fs_l = [], [], [], []
    for d in range(DEPTH):
        params = (w_ada[d], b_ada[d], g_pre_mix[d], g_post_mix[d], w_in[d], sb_bias[d], conv_w[d], g_attn_out[d],
                  g_conv_out[d], w_o[d], g_pre_ffn[d], g_post_ffn[d], w_fc1[d], ffn_conv_w[d], w_fc2[d])
        conv0 = jnp.zeros((bp, CONV_WIDTH - 1, D_CONV), xp.dtype)
        ffn0 = jnp.zeros((bp, CONV_WIDTH - 1, D_FF), xp.dtype)
        xp, kp, vp, cp, fp = decoder_layer(xp, c_prompt, prompt_attention, conv0, ffn0, *params)
        attend = make_paged_attention(cache_k[d], cache_v[d], page_table)
        xs, k_s, v_s, c_s, f_s = decoder_layer(xs, c_sample, attend, state_conv[d], state_ffn_conv[d], *params)
        kp_l.append(kp); vp_l.append(vp); cp_l.append(cp); fp_l.append(fp)
        ks_l.append(k_s); vs_l.append(v_s); cs_l.append(c_s); fs_l.append(f_s)
    return (xp, xs, jnp.stack(kp_l), jnp.stack(vp_l), jnp.stack(cp_l), jnp.stack(fp_l),
            jnp.stack(ks_l), jnp.stack(vs_l), jnp.stack(cs_l), jnp.stack(fs_l))
```

```python
import functools

import jax
import jax.numpy as jnp
from jax import lax
from jax.experimental import pallas as pl
from jax.experimental.pallas import tpu as pltpu

F32 = jnp.float32
BF16 = jnp.bfloat16
EPS = 1e-6
CONV_WIDTH = 3
V7X_VMEM_LIMIT_BYTES = 56 * 1024 * 1024
LANE = 128
NT_DIMS = (((1,), (1,)), ((), ()))


def _params(n_axes, vmem=V7X_VMEM_LIMIT_BYTES):
    return pltpu.CompilerParams(dimension_semantics=("arbitrary",) * n_axes, vmem_limit_bytes=vmem)


def _tile(dim, pref):
    if dim <= pref:
        return dim
    t = (pref // LANE) * LANE
    while dim % t:
        t -= LANE
    return t


def _rms(x):
    return x * lax.rsqrt(jnp.mean(x * x, axis=-1, keepdims=True) + EPS)


def _silu(x):
    return x * jax.nn.sigmoid(x)


def _ada_kernel(c_ref, w_ref, b_ref, o_ref, a_ref):
    @pl.when(pl.program_id(0) == 0)
    def _():
        a_ref[...] = _silu(c_ref[...]).astype(BF16)

    o_ref[...] = jnp.dot(a_ref[...], w_ref[...].astype(BF16), preferred_element_type=F32) + b_ref[...]


def _ada(c_all, w, b):
    rows, d = c_all.shape
    n = w.shape[1]
    tn = _tile(n, 512)
    return pl.pallas_call(
        _ada_kernel,
        grid=(n // tn,),
        in_specs=[
            pl.BlockSpec((rows, d), lambda j: (0, 0)),
            pl.BlockSpec((d, tn), lambda j: (0, j)),
            pl.BlockSpec((1, tn), lambda j: (0, j)),
        ],
        out_specs=pl.BlockSpec((rows, tn), lambda j: (0, j)),
        out_shape=jax.ShapeDtypeStruct((rows, n), F32),
        scratch_shapes=[pltpu.VMEM((rows, d), BF16)],
        compiler_params=_params(1),
        name="ada_mod",
    )(c_all, w, b)


class _Rows:
    def __init__(self, n_rows, tm, tiles_per_group, mod):
        self.n_rows, self.tm, self.tpg, self.mod = n_rows, tm, tiles_per_group, mod
        self.grid = (n_rows // tm,)

    def row(self, width, col=0):
        return pl.BlockSpec((self.tm, width), lambda i: (i, col))

    def vec(self, width):
        return pl.BlockSpec((1, width), lambda i: (0, 0))

    def modv(self, chunk, width):
        tpg = self.tpg
        return pl.BlockSpec((None, self.mod.shape[1], width), lambda i: (i // tpg, 0, chunk))


def _prenorm_kernel(x_ref, g_ref, sh_ref, sc_ref, h_ref):
    y = _rms(x_ref[...]) * g_ref[...]
    h_ref[...] = (y * (1.0 + sc_ref[...]) + sh_ref[...]).astype(h_ref.dtype)


def _prenorm(rows, x, g, sh_chunk, sc_chunk):
    d = x.shape[1]
    return pl.pallas_call(
        _prenorm_kernel,
        grid=rows.grid,
        in_specs=[rows.row(d), rows.vec(d), rows.modv(sh_chunk, d), rows.modv(sc_chunk, d)],
        out_specs=rows.row(d),
        out_shape=jax.ShapeDtypeStruct(x.shape, BF16),
        compiler_params=_params(1),
        name="prenorm",
    )(x, g, rows.mod, rows.mod)


def _merge_norm_kernel(att_ref, short_ref, ga_ref, gc_ref, o_ref):
    da = att_ref.shape[1]
    o_ref[:, :da] = (_rms(att_ref[...]) * ga_ref[...]).astype(o_ref.dtype)
    o_ref[:, da:] = (_rms(short_ref[...]) * gc_ref[...]).astype(o_ref.dtype)


def _merge_norm(rows, att, short, ga, gc):
    da, dc = att.shape[1], short.shape[1]
    return pl.pallas_call(
        _merge_norm_kernel,
        grid=rows.grid,
        in_specs=[rows.row(da), rows.row(dc), rows.vec(da), rows.vec(dc)],
        out_specs=rows.row(da + dc),
        out_shape=jax.ShapeDtypeStruct((att.shape[0], da + dc), BF16),
        compiler_params=_params(1),
        name="merge_norm",
    )(att, short, ga, gc)


def _resid_prenorm_kernel(x_ref, o_ref, gpost_ref, gt_ref, gpre_ref, sh_ref, sc_ref, x1_ref, h_ref):
    x1 = x_ref[...] + gt_ref[...] * (_rms(o_ref[...]) * gpost_ref[...])
    x1_ref[...] = x1
    y = _rms(x1) * gpre_ref[...]
    h_ref[...] = (y * (1.0 + sc_ref[...]) + sh_ref[...]).astype(h_ref.dtype)


def _resid_prenorm(rows, x, o, gpost, gt_chunk, gpre, sh_chunk, sc_chunk):
    d = x.shape[1]
    return pl.pallas_call(
        _resid_prenorm_kernel,
        grid=rows.grid,
        in_specs=[rows.row(d), rows.row(d), rows.vec(d), rows.modv(gt_chunk, d), rows.vec(d),
                  rows.modv(sh_chunk, d), rows.modv(sc_chunk, d)],
        out_specs=[rows.row(d), rows.row(d)],
        out_shape=[jax.ShapeDtypeStruct(x.shape, F32), jax.ShapeDtypeStruct(x.shape, BF16)],
        compiler_params=_params(1),
        name="resid_prenorm",
    )(x, o, gpost, rows.mod, gpre, rows.mod, rows.mod)


def _resid_kernel(x_ref, f_ref, gpost_ref, gt_ref, y_ref):
    y_ref[...] = x_ref[...] + gt_ref[...] * (_rms(f_ref[...]) * gpost_ref[...])


def _resid(rows, x, f, gpost, gt_chunk):
    d = x.shape[1]
    return pl.pallas_call(
        _resid_kernel,
        grid=rows.grid,
        in_specs=[rows.row(d), rows.row(d), rows.vec(d), rows.modv(gt_chunk, d)],
        out_specs=rows.row(d),
        out_shape=jax.ShapeDtypeStruct(x.shape, F32),
        compiler_params=_params(1),
        name="resid",
    )(x, f, gpost, rows.mod)


def _mm_kernel(a_ref, w_ref, o_ref, *, nk):
    acc = jnp.dot(a_ref[...], w_ref[...], preferred_element_type=F32)
    if nk == 1:
        o_ref[...] = acc.astype(o_ref.dtype)
    else:
        k = pl.program_id(2)

        @pl.when(k == 0)
        def _():
            o_ref[...] = acc

        @pl.when(k != 0)
        def _():
            o_ref[...] += acc


def _matmul(a, w, *, col0=0, ncols=None, tm_pref=1024, tn_pref=512, tk_pref=5632, name="matmul"):
    m, kdim = a.shape
    ncols = w.shape[1] if ncols is None else ncols
    tm, tn, tk = _tile(m, tm_pref), _tile(ncols, tn_pref), _tile(kdim, tk_pref)
    assert col0 % tn == 0
    cb, nk = col0 // tn, kdim // tk
    return pl.pallas_call(
        functools.partial(_mm_kernel, nk=nk),
        grid=(m // tm, ncols // tn, nk),
        in_specs=[
            pl.BlockSpec((tm, tk), lambda i, j, k: (i, k)),
            pl.BlockSpec((tk, tn), lambda i, j, k: (k, j + cb)),
        ],
        out_specs=pl.BlockSpec((tm, tn), lambda i, j, k: (i, j)),
        out_shape=jax.ShapeDtypeStruct((m, ncols), F32),
        compiler_params=_params(3),
        name=name,
    )(a, w)


def _stick_block(q, k, v, u, z_bias, scale, c, acc, valid):
    s = lax.dot_general(q, k, NT_DIMS, preferred_element_type=F32)
    z = s * scale + z_bias
    softplus = jnp.maximum(z, 0.0) + jnp.log(1.0 + jnp.exp(-jnp.abs(z)))
    log_keep = -softplus
    if valid is not None:
        log_keep = jnp.where(valid, log_keep, 0.0)
    hi = log_keep.astype(BF16)
    lo = (log_keep - hi.astype(F32)).astype(BF16)
    log_after = (jnp.dot(hi, u, preferred_element_type=F32) + jnp.dot(lo, u, preferred_element_type=F32)) + c
    a = jnp.exp((z - softplus) + log_after)
    if valid is not None:
        a = jnp.where(valid, a, 0.0)
    acc = acc + jnp.dot(a.astype(BF16), v, preferred_element_type=F32)
    c = c + jnp.sum(log_keep, axis=-1, keepdims=True)
    return c, acc


def _strict_lower(n):
    r = lax.broadcasted_iota(jnp.int32, (n, n), 0)
    c = lax.broadcasted_iota(jnp.int32, (n, n), 1)
    return (r > c).astype(BF16)


def _pattn_kernel(bias_ref, q_ref, k_ref, v_ref, u_ref, o_ref, kb_ref, vb_ref, *, blk, scale):
    h, qi = pl.program_id(1), pl.program_id(2)

    @pl.when(qi == 0)
    def _():
        kb_ref[...] = k_ref[...].astype(BF16)
        vb_ref[...] = v_ref[...].astype(BF16)

    q = q_ref[...].astype(BF16)
    u = u_ref[...]
    bias = bias_ref[h]
    dh = q.shape[1]

    def visit(kb, c, acc, valid):
        start = pl.multiple_of(kb * blk, blk)
        k = kb_ref[pl.ds(start, blk), :]
        v = vb_ref[pl.ds(start, blk), :]
        return _stick_block(q, k, v, u, bias, scale, c, acc, valid)

    row = lax.broadcasted_iota(jnp.int32, (blk, blk), 0)
    col = lax.broadcasted_iota(jnp.int32, (blk, blk), 1)
    c0 = jnp.zeros((blk, 1), F32)
    acc0 = jnp.zeros((blk, dh), F32)
    c, acc = visit(qi, c0, acc0, col < row)

    def body(it, carry):
        return visit(qi - 1 - it, carry[0], carry[1], None)

    c, acc = lax.fori_loop(0, qi, body, (c, acc))
    o_ref[...] = acc


def _prompt_attention(q, k, v, bias, nh, dh):
    b, t, da = q.shape
    blk = _tile(t, 256)
    u = _strict_lower(blk)
    qspec = pl.BlockSpec((None, blk, dh), lambda n, h, i: (n, i, h))
    kvspec = pl.BlockSpec((None, t, dh), lambda n, h, i: (n, 0, h))
    return pl.pallas_call(
        functools.partial(_pattn_kernel, blk=blk, scale=dh ** -0.5),
        grid=(b, nh, t // blk),
        in_specs=[
            pl.BlockSpec(memory_space=pltpu.SMEM),
            qspec, kvspec, kvspec,
            pl.BlockSpec((blk, blk), lambda n, h, i: (0, 0)),
        ],
        out_specs=qspec,
        out_shape=jax.ShapeDtypeStruct((b, t, da), F32),
        scratch_shapes=[pltpu.VMEM((t, dh), BF16), pltpu.VMEM((t, dh), BF16)],
        compiler_params=_params(3),
        name="prompt_attn",
    )(bias, q, k, v, u)


def _sattn_kernel(pt_ref, q_ref, kn_ref, vn_ref, k0_ref, k1_ref, v0_ref, v1_ref, bias_ref, u_ref, o_ref,
                  qbd_ref, c_ref, acc_ref, kpad_ref, vpad_ref, *, nh, dh, nt, page, scale):
    p = pl.program_id(1)
    da = nh * dh
    nrows = nt * nh
    log_nh, log_dh = nh.bit_length() - 1, dh.bit_length() - 1
    r = lax.broadcasted_iota(jnp.int32, (nrows, da), 0)
    lane = lax.broadcasted_iota(jnp.int32, (nrows, da), 1)
    own_head = (lane >> log_dh) == (r & (nh - 1))

    @pl.when(p == 0)
    def _():
        qrep = jnp.concatenate([jnp.broadcast_to(q_ref[t:t + 1, :], (nh, da)) for t in range(nt)], axis=0)
        qbd_ref[...] = jnp.where(own_head, qrep, 0.0).astype(BF16)
        kpad_ref[...] = jnp.zeros(kpad_ref.shape, F32)
        vpad_ref[...] = jnp.zeros(vpad_ref.shape, F32)
        kpad_ref[0:nt, :] = kn_ref[...]
        vpad_ref[0:nt, :] = vn_ref[...]
        rr = lax.broadcasted_iota(jnp.int32, (nrows, page), 0)
        cc = lax.broadcasted_iota(jnp.int32, (nrows, page), 1)
        valid = cc < (rr >> log_nh)
        c, acc = _stick_block(qbd_ref[...], kpad_ref[...].astype(BF16), vpad_ref[...].astype(BF16),
                              u_ref[0:page, 0:page], bias_ref[:, 0:page], scale,
                              jnp.zeros((nrows, 1), F32), jnp.zeros((nrows, da), F32), valid)
        c_ref[...] = c
        acc_ref[...] = acc

    kcat = jnp.concatenate([k0_ref[...].astype(BF16), k1_ref[...].astype(BF16)], axis=0)
    vcat = jnp.concatenate([v0_ref[...].astype(BF16), v1_ref[...].astype(BF16)], axis=0)
    c, acc = _stick_block(qbd_ref[...], kcat, vcat, u_ref[...], bias_ref[...], scale,
                          c_ref[...], acc_ref[...], None)
    c_ref[...] = c
    acc_ref[...] = acc

    @pl.when(p == pl.num_programs(1) - 1)
    def _():
        own = jnp.where(own_head, acc, 0.0)
        o_ref[...] = jnp.sum(own.reshape(nt, nh, da), axis=1)


def _sample_attention(q, k_new, v_new, k_pages, v_pages, page_table, bias, nh, dh):
    ns, nt, da = q.shape
    page = k_pages.shape[1]
    npg = page_table.shape[1]
    assert npg % 2 == 0 and nh & (nh - 1) == 0 and dh & (dh - 1) == 0 and nt <= page
    nrows = nt * nh
    u = _strict_lower(2 * page)
    bias_rows = jnp.broadcast_to(jnp.tile(bias.astype(F32), nt)[:, None], (nrows, 2 * page))
    pt = page_table.reshape(-1).astype(jnp.int32)

    seq = pl.BlockSpec((None, nt, da), lambda n, p, pt: (n, 0, 0))

    def page_spec(which):
        return pl.BlockSpec((None, page, da), lambda n, p, pt: (pt[n * npg + npg - 2 - 2 * p + which], 0, 0))

    const = lambda shape: pl.BlockSpec(shape, lambda n, p, pt: (0, 0))
    grid_spec = pltpu.PrefetchScalarGridSpec(
        num_scalar_prefetch=1,
        grid=(ns, npg // 2),
        in_specs=[seq, seq, seq, page_spec(0), page_spec(1), page_spec(0), page_spec(1),
                  const((nrows, 2 * page)), const((2 * page, 2 * page))],
        out_specs=seq,
        scratch_shapes=[
            pltpu.VMEM((nrows, da), BF16),
            pltpu.VMEM((nrows, 1), F32),
            pltpu.VMEM((nrows, da), F32),
            pltpu.VMEM((page, da), F32),
            pltpu.VMEM((page, da), F32),
        ],
    )
    return pl.pallas_call(
        functools.partial(_sattn_kernel, nh=nh, dh=dh, nt=nt, page=page, scale=dh ** -0.5),
        grid_spec=grid_spec,
        out_shape=jax.ShapeDtypeStruct((ns, nt, da), F32),
        compiler_params=_params(2),
        name="sample_attn",
    )(pt, q, k_new, v_new, k_pages, k_pages, v_pages, v_pages, bias_rows, u)


def _shift_rows(x, s):
    row = lax.broadcasted_iota(jnp.int32, x.shape, 0)
    return jnp.where(row >= s, pltpu.roll(x, s, 0), 0.0)


def _conv_rows(u, w_ref):
    return w_ref[0:1, :] * _shift_rows(u, 2) + w_ref[1:2, :] * _shift_rows(u, 1) + w_ref[2:3, :] * u


def _conv_time_major(us, prev_ref, w_ref):
    seq = [prev_ref[0], prev_ref[1]] + us
    ys = [w_ref[0:1, :] * seq[t] + w_ref[1:2, :] * seq[t + 1] + w_ref[2:3, :] * seq[t + 2] for t in range(len(us))]
    return ys, seq[-2:]


def _convgate_p_kernel(b_ref, c_ref, u_ref, w_ref, short_ref, st_ref):
    t = c_ref.shape[0]
    st_ref[...] = c_ref[t - 2:t, :] * u_ref[t - 2:t, :]
    short_ref[...] = b_ref[...] * _conv_rows(c_ref[...] * u_ref[...], w_ref)


def _convgate_prompt(pg, conv_w, n_seq, t, dc):
    tc = _tile(dc, 256)
    nj = dc // tc
    return pl.pallas_call(
        _convgate_p_kernel,
        grid=(n_seq, nj),
        in_specs=[
            pl.BlockSpec((t, tc), lambda n, j: (n, j)),
            pl.BlockSpec((t, tc), lambda n, j: (n, j + nj)),
            pl.BlockSpec((t, tc), lambda n, j: (n, j + 2 * nj)),
            pl.BlockSpec((CONV_WIDTH, tc), lambda n, j: (0, j)),
        ],
        out_specs=[pl.BlockSpec((t, tc), lambda n, j: (n, j)), pl.BlockSpec((None, 2, tc), lambda n, j: (n, 0, j))],
        out_shape=[jax.ShapeDtypeStruct((n_seq * t, dc), F32), jax.ShapeDtypeStruct((n_seq, 2, dc), F32)],
        compiler_params=_params(2),
        name="convgate_prompt",
    )(pg, pg, pg, conv_w)


def _ffn_p_kernel(a_ref, b_ref, w_ref, g_ref, st_ref):
    t = a_ref.shape[0]
    st_ref[...] = a_ref[t - 2:t, :]
    g_ref[...] = (_silu(_conv_rows(a_ref[...], w_ref)) * b_ref[...]).astype(g_ref.dtype)


def _ffn_gate_prompt(ab, conv_w, n_seq, t, dff):
    tc = _tile(dff, 256)
    nj = dff // tc
    return pl.pallas_call(
        _ffn_p_kernel,
        grid=(n_seq, nj),
        in_specs=[
            pl.BlockSpec((t, tc), lambda n, j: (n, j)),
            pl.BlockSpec((t, tc), lambda n, j: (n, j + nj)),
            pl.BlockSpec((CONV_WIDTH, tc), lambda n, j: (0, j)),
        ],
        out_specs=[pl.BlockSpec((t, tc), lambda n, j: (n, j)), pl.BlockSpec((None, 2, tc), lambda n, j: (n, 0, j))],
        out_shape=[jax.ShapeDtypeStruct((n_seq * t, dff), BF16), jax.ShapeDtypeStruct((n_seq, 2, dff), F32)],
        compiler_params=_params(2),
        name="ffn_gate_prompt",
    )(ab, ab, conv_w)


def _convgate_s_kernel(b_ref, c_ref, u_ref, prev_ref, w_ref, short_ref, st_ref, *, nt):
    ns = prev_ref.shape[1]
    cu = [c_ref[t * ns:(t + 1) * ns, :] * u_ref[t * ns:(t + 1) * ns, :] for t in range(nt)]
    ys, last = _conv_time_major(cu, prev_ref, w_ref)
    for t in range(nt):
        short_ref[t * ns:(t + 1) * ns, :] = b_ref[t * ns:(t + 1) * ns, :] * ys[t]
    st_ref[0] = last[0]
    st_ref[1] = last[1]


def _convgate_sample(pg, prev, conv_w, nt, ns, dc):
    tc = _tile(dc, 512)
    nj = dc // tc
    rows = nt * ns
    return pl.pallas_call(
        functools.partial(_convgate_s_kernel, nt=nt),
        grid=(nj,),
        in_specs=[
            pl.BlockSpec((rows, tc), lambda j: (0, j)),
            pl.BlockSpec((rows, tc), lambda j: (0, j + nj)),
            pl.BlockSpec((rows, tc), lambda j: (0, j + 2 * nj)),
            pl.BlockSpec((2, ns, tc), lambda j: (0, 0, j)),
            pl.BlockSpec((CONV_WIDTH, tc), lambda j: (0, j)),
        ],
        out_specs=[pl.BlockSpec((rows, tc), lambda j: (0, j)), pl.BlockSpec((2, ns, tc), lambda j: (0, 0, j))],
        out_shape=[jax.ShapeDtypeStruct((rows, dc), F32), jax.ShapeDtypeStruct((2, ns, dc), F32)],
        compiler_params=_params(1),
        name="convgate_sample",
    )(pg, pg, pg, prev, conv_w)


def _ffn_s_kernel(a_ref, b_ref, prev_ref, w_ref, g_ref, st_ref, *, nt):
    ns = prev_ref.shape[1]
    a = [a_ref[t * ns:(t + 1) * ns, :] for t in range(nt)]
    ys, last = _conv_time_major(a, prev_ref, w_ref)
    for t in range(nt):
        g_ref[t * ns:(t + 1) * ns, :] = (_silu(ys[t]) * b_ref[t * ns:(t + 1) * ns, :]).astype(g_ref.dtype)
    st_ref[0] = last[0]
    st_ref[1] = last[1]


def _ffn_gate_sample(ab, prev, conv_w, nt, ns, dff):
    tc = _tile(dff, 256)
    nj = dff // tc
    rows = nt * ns
    return pl.pallas_call(
        functools.partial(_ffn_s_kernel, nt=nt),
        grid=(nj,),
        in_specs=[
            pl.BlockSpec((rows, tc), lambda j: (0, j)),
            pl.BlockSpec((rows, tc), lambda j: (0, j + nj)),
            pl.BlockSpec((2, ns, tc), lambda j: (0, 0, j)),
            pl.BlockSpec((CONV_WIDTH, tc), lambda j: (0, j)),
        ],
        out_specs=[pl.BlockSpec((rows, tc), lambda j: (0, j)), pl.BlockSpec((2, ns, tc), lambda j: (0, 0, j))],
        out_shape=[jax.ShapeDtypeStruct((rows, dff), BF16), jax.ShapeDtypeStruct((2, ns, dff), F32)],
        compiler_params=_params(1),
        name="ffn_gate_sample",
    )(ab, ab, prev, conv_w)


MOD_SH1, MOD_SC1, MOD_GT1, MOD_SH2, MOD_SC2, MOD_GT2 = range(6)


def _layer(x, rows, attend, convgate, ffn_gate, wts, da, dc, dff):
    h = _prenorm(rows, x, wts["g_pre_mix"], MOD_SH1, MOD_SC1)
    q = _matmul(h, wts["w_in"], col0=0, ncols=da, name="proj_q")
    k = _matmul(h, wts["w_in"], col0=da, ncols=da, name="proj_k")
    v = _matmul(h, wts["w_in"], col0=2 * da, ncols=da, name="proj_v")
    pg = _matmul(h, wts["w_in"], col0=3 * da, ncols=3 * dc, name="proj_conv")
    att = attend(q, k, v)
    short, conv_state = convgate(pg)
    merged = _merge_norm(rows, att, short, wts["g_attn_out"], wts["g_conv_out"])
    o = _matmul(merged, wts["w_o"], name="out_proj")
    x1, h2 = _resid_prenorm(rows, x, o, wts["g_post_mix"], MOD_GT1, wts["g_pre_ffn"], MOD_SH2, MOD_SC2)
    ab = _matmul(h2, wts["w_fc1"], name="fc1")
    g, ffn_state = ffn_gate(ab)
    f = _matmul(g, wts["w_fc2"], name="fc2")
    y = _resid(rows, x1, f, wts["g_post_ffn"], MOD_GT2)
    return y, k, v, conv_state, ffn_state


def kernel(x_prompt, x_sample, c_prompt, c_sample, cache_k, cache_v, state_conv, state_ffn_conv, page_table, w_ada, b_ada, g_pre_mix, g_post_mix, w_in, sb_bias, conv_w, g_attn_out, g_conv_out, w_o, g_pre_ffn, g_post_ffn, w_fc1, ffn_conv_w, w_fc2):
    b, t, d = x_prompt.shape
    ns, nt, _ = x_sample.shape
    depth, n_phys, page, nh, dh = cache_k.shape
    assert depth == 1 and conv_w.shape[1] == CONV_WIDTH and ffn_conv_w.shape[1] == CONV_WIDTH
    da, dc, dff = nh * dh, conv_w.shape[-1], ffn_conv_w.shape[-1]

    wts = {
        "w_in": w_in.reshape(d, -1).astype(BF16),
        "w_o": w_o.reshape(da + dc, d).astype(BF16),
        "w_fc1": w_fc1.reshape(d, 2 * dff).astype(BF16),
        "w_fc2": w_fc2.reshape(dff, d).astype(BF16),
        "g_pre_mix": g_pre_mix.reshape(1, d), "g_post_mix": g_post_mix.reshape(1, d),
        "g_attn_out": g_attn_out.reshape(1, da), "g_conv_out": g_conv_out.reshape(1, dc),
        "g_pre_ffn": g_pre_ffn.reshape(1, d), "g_post_ffn": g_post_ffn.reshape(1, d),
    }
    cw, fcw = conv_w.reshape(CONV_WIDTH, dc), ffn_conv_w.reshape(CONV_WIDTH, dff)
    bias = sb_bias.reshape(nh)

    n_c = ns + b
    pad = -n_c % 16
    c_all = jnp.concatenate([c_sample, c_prompt, jnp.zeros((pad, d), F32)], axis=0)
    mod = _ada(c_all, w_ada.reshape(d, 6 * d), b_ada.reshape(1, 6 * d))
    mod_s = mod[:ns].reshape(1, ns, 6 * d)
    mod_p = mod[ns:n_c].reshape(b, 1, 6 * d)

    tm_p = _tile(t, 256)
    rows_p = _Rows(b * t, tm_p, t // tm_p, mod_p)

    def attend_p(q, k, v):
        r3 = lambda a: a.reshape(b, t, da)
        return _prompt_attention(r3(q), r3(k), r3(v), bias, nh, dh).reshape(b * t, da)

    yp, kp, vp, cp, fp = _layer(
        x_prompt.reshape(b * t, d), rows_p, attend_p,
        lambda pg: _convgate_prompt(pg, cw, b, t, dc),
        lambda ab: _ffn_gate_prompt(ab, fcw, b, t, dff),
        wts, da, dc, dff)

    rows_s = _Rows(nt * ns, ns, nt, mod_s)
    to_seq_major = lambda a: a.reshape(nt, ns, -1).transpose(1, 0, 2)
    to_time_major = lambda a: a.transpose(1, 0, 2).reshape(nt * ns, -1)
    kv_seq = {}

    def attend_s(q, k, v):
        kv_seq["k"], kv_seq["v"] = to_seq_major(k), to_seq_major(v)
        att = _sample_attention(to_seq_major(q), kv_seq["k"], kv_seq["v"],
                                cache_k.reshape(n_phys, page, da), cache_v.reshape(n_phys, page, da),
                                page_table, bias, nh, dh)
        return to_time_major(att)

    prev_conv = state_conv.reshape(ns, CONV_WIDTH - 1, dc).transpose(1, 0, 2)
    prev_ffn = state_ffn_conv.reshape(ns, CONV_WIDTH - 1, dff).transpose(1, 0, 2)
    ys, _, _, cs, fs = _layer(
        to_time_major(x_sample), rows_s, attend_s,
        lambda pg: _convgate_sample(pg, prev_conv, cw, nt, ns, dc),
        lambda ab: _ffn_gate_sample(ab, prev_ffn, fcw, nt, ns, dff),
        wts, da, dc, dff)

    return (
        yp.reshape(b, t, d),
        to_seq_major(ys),
        kp.reshape(1, b, t, nh, dh),
        vp.reshape(1, b, t, nh, dh),
        cp.reshape(1, b, CONV_WIDTH - 1, dc),
        fp.reshape(1, b, CONV_WIDTH - 1, dff),
        kv_seq["k"].reshape(1, ns, nt, nh, dh),
        kv_seq["v"].reshape(1, ns, nt, nh, dh),
        cs.transpose(1, 0, 2).reshape(1, ns, CONV_WIDTH - 1, dc),
        fs.transpose(1, 0, 2).reshape(1, ns, CONV_WIDTH - 1, dff),
    )
```

```python
import functools

import jax
import jax.numpy as jnp
from jax import lax
from jax.experimental import pallas as pl
from jax.experimental.pallas import tpu as pltpu

F32 = jnp.float32
BF16 = jnp.bfloat16
EPS = 1e-6
CONV_WIDTH = 3
V7X_VMEM_LIMIT_BYTES = 56 * 1024 * 1024
LANE = 128
NT_DIMS = (((1,), (1,)), ((), ()))


def _params(n_axes, vmem=V7X_VMEM_LIMIT_BYTES):
    return pltpu.CompilerParams(dimension_semantics=("arbitrary",) * n_axes, vmem_limit_bytes=vmem)


def _tile(dim, pref):
    if dim <= pref:
        return dim
    t = (pref // LANE) * LANE
    while dim % t:
        t -= LANE
    return t


def _rms(x):
    return x * lax.rsqrt(jnp.mean(x * x, axis=-1, keepdims=True) + EPS)


def _silu(x):
    return x * jax.nn.sigmoid(x)


def _ada_kernel(c_ref, w_ref, b_ref, o_ref, a_ref):
    @pl.when(pl.program_id(0) == 0)
    def _():
        a_ref[...] = _silu(c_ref[...]).astype(BF16)

    o_ref[...] = jnp.dot(a_ref[...], w_ref[...].astype(BF16), preferred_element_type=F32) + b_ref[...]


def _ada(c_all, w, b):
    rows, d = c_all.shape
    n = w.shape[1]
    tn = _tile(n, 512)
    return pl.pallas_call(
        _ada_kernel,
        grid=(n // tn,),
        in_specs=[
            pl.BlockSpec((rows, d), lambda j: (0, 0)),
            pl.BlockSpec((d, tn), lambda j: (0, j)),
            pl.BlockSpec((1, tn), lambda j: (0, j)),
        ],
        out_specs=pl.BlockSpec((rows, tn), lambda j: (0, j)),
        out_shape=jax.ShapeDtypeStruct((rows, n), F32),
        scratch_shapes=[pltpu.VMEM((rows, d), BF16)],
        compiler_params=_params(1),
        name="ada_mod",
    )(c_all, w, b)


class _Rows:
    def __init__(self, n_rows, tm, tiles_per_group, mod):
        self.n_rows, self.tm, self.tpg, self.mod = n_rows, tm, tiles_per_group, mod
        self.grid = (n_rows // tm,)

    def row(self, width, col=0):
        return pl.BlockSpec((self.tm, width), lambda i: (i, col))

    def vec(self, width):
        return pl.BlockSpec((1, width), lambda i: (0, 0))

    def modv(self, chunk, width):
        tpg = self.tpg
        return pl.BlockSpec((None, self.mod.shape[1], width), lambda i: (i // tpg, 0, chunk))


def _prenorm_kernel(x_ref, g_ref, sh_ref, sc_ref, h_ref):
    y = _rms(x_ref[...]) * g_ref[...]
    h_ref[...] = (y * (1.0 + sc_ref[...]) + sh_ref[...]).astype(h_ref.dtype)


def _prenorm(rows, x, g, sh_chunk, sc_chunk):
    d = x.shape[1]
    return pl.pallas_call(
        _prenorm_kernel,
        grid=rows.grid,
        in_specs=[rows.row(d), rows.vec(d), rows.modv(sh_chunk, d), rows.modv(sc_chunk, d)],
        out_specs=rows.row(d),
        out_shape=jax.ShapeDtypeStruct(x.shape, BF16),
        compiler_params=_params(1),
        name="prenorm",
    )(x, g, rows.mod, rows.mod)


def _merge_norm_kernel(att_ref, short_ref, ga_ref, gc_ref, o_ref):
    da = att_ref.shape[1]
    o_ref[:, :da] = (_rms(att_ref[...]) * ga_ref[...]).astype(o_ref.dtype)
    o_ref[:, da:] = (_rms(short_ref[...]) * gc_ref[...]).astype(o_ref.dtype)


def _merge_norm(rows, att, short, ga, gc):
    da, dc = att.shape[1], short.shape[1]
    return pl.pallas_call(
        _merge_norm_kernel,
        grid=rows.grid,
        in_specs=[rows.row(da), rows.row(dc), rows.vec(da), rows.vec(dc)],
        out_specs=rows.row(da + dc),
        out_shape=jax.ShapeDtypeStruct((att.shape[0], da + dc), BF16),
        compiler_params=_params(1),
        name="merge_norm",
    )(att, short, ga, gc)


def _resid_prenorm_kernel(x_ref, o_ref, gpost_ref, gt_ref, gpre_ref, sh_ref, sc_ref, x1_ref, h_ref):
    x1 = x_ref[...] + gt_ref[...] * (_rms(o_ref[...]) * gpost_ref[...])
    x1_ref[...] = x1
    y = _rms(x1) * gpre_ref[...]
    h_ref[...] = (y * (1.0 + sc_ref[...]) + sh_ref[...]).astype(h_ref.dtype)


def _resid_prenorm(rows, x, o, gpost, gt_chunk, gpre, sh_chunk, sc_chunk):
    d = x.shape[1]
    return pl.pallas_call(
        _resid_prenorm_kernel,
        grid=rows.grid,
        in_specs=[rows.row(d), rows.row(d), rows.vec(d), rows.modv(gt_chunk, d), rows.vec(d),
                  rows.modv(sh_chunk, d), rows.modv(sc_chunk, d)],
        out_specs=[rows.row(d), rows.row(d)],
        out_shape=[jax.ShapeDtypeStruct(x.shape, F32), jax.ShapeDtypeStruct(x.shape, BF16)],
        compiler_params=_params(1),
        name="resid_prenorm",
    )(x, o, gpost, rows.mod, gpre, rows.mod, rows.mod)


def _resid_kernel(x_ref, f_ref, gpost_ref, gt_ref, y_ref):
    y_ref[...] = x_ref[...] + gt_ref[...] * (_rms(f_ref[...]) * gpost_ref[...])


def _resid(rows, x, f, gpost, gt_chunk):
    d = x.shape[1]
    return pl.pallas_call(
        _resid_kernel,
        grid=rows.grid,
        in_specs=[rows.row(d), rows.row(d), rows.vec(d), rows.modv(gt_chunk, d)],
        out_specs=rows.row(d),
        out_shape=jax.ShapeDtypeStruct(x.shape, F32),
        compiler_params=_params(1),
        name="resid",
    )(x, f, gpost, rows.mod)


def _mm_kernel(a_ref, w_ref, o_ref, *, nk):
    acc = jnp.dot(a_ref[...], w_ref[...], preferred_element_type=F32)
    if nk == 1:
        o_ref[...] = acc.astype(o_ref.dtype)
    else:
        k = pl.program_id(2)

        @pl.when(k == 0)
        def _():
            o_ref[...] = acc

        @pl.when(k != 0)
        def _():
            o_ref[...] += acc


def _matmul(a, w, *, col0=0, ncols=None, tm_pref=1024, tn_pref=512, tk_pref=5632, name="matmul"):
    m, kdim = a.shape
    ncols = w.shape[1] if ncols is None else ncols
    tm, tn, tk = _tile(m, tm_pref), _tile(ncols, tn_pref), _tile(kdim, tk_pref)
    assert col0 % tn == 0
    cb, nk = col0 // tn, kdim // tk
    return pl.pallas_call(
        functools.partial(_mm_kernel, nk=nk),
        grid=(m // tm, ncols // tn, nk),
        in_specs=[
            pl.BlockSpec((tm, tk), lambda i, j, k: (i, k)),
            pl.BlockSpec((tk, tn), lambda i, j, k: (k, j + cb)),
        ],
        out_specs=pl.BlockSpec((tm, tn), lambda i, j, k: (i, j)),
        out_shape=jax.ShapeDtypeStruct((m, ncols), F32),
        compiler_params=_params(3),
        name=name,
    )(a, w)


def _stick_chains(chains, u):
    items = [(ci, bi) for ci, ch in enumerate(chains) for bi in range(len(ch[1]))]
    z, sp, lk, lu, a = {}, {}, {}, {}, {}
    for ci, bi in items:
        q, ks, _, bias, _, _, _ = chains[ci]
        z[ci, bi] = lax.dot_general(q, ks[bi], NT_DIMS, preferred_element_type=F32) + bias
    for it in items:
        sp[it] = jnp.maximum(z[it], 0.0) + jnp.log(1.0 + jnp.exp(-jnp.abs(z[it])))
        valid = chains[it[0]][6][it[1]]
        lk[it] = -sp[it] if valid is None else jnp.where(valid, -sp[it], 0.0)
    hi = {it: lk[it].astype(BF16) for it in items}
    lo = {it: (lk[it] - hi[it].astype(F32)).astype(BF16) for it in items}
    for it in items:
        lu[it] = jnp.dot(hi[it], u, preferred_element_type=F32) + jnp.dot(lo[it], u, preferred_element_type=F32)
    cs = [ch[4] for ch in chains]
    for it in items:
        ci = it[0]
        valid = chains[ci][6][it[1]]
        e = jnp.exp((z[it] - sp[it]) + (lu[it] + cs[ci]))
        a[it] = (e if valid is None else jnp.where(valid, e, 0.0)).astype(BF16)
        cs[ci] = cs[ci] + jnp.sum(lk[it], axis=-1, keepdims=True)
    accs = [ch[5] for ch in chains]
    for ci, bi in items:
        accs[ci] = accs[ci] + jnp.dot(a[ci, bi], chains[ci][2][bi], preferred_element_type=F32)
    return list(zip(cs, accs))


def _strict_lower(n):
    r = lax.broadcasted_iota(jnp.int32, (n, n), 0)
    c = lax.broadcasted_iota(jnp.int32, (n, n), 1)
    return (r > c).astype(BF16)


PATTN_HEADS_PER_STEP = 4


def _pattn_kernel(bias_ref, q_ref, k_ref, v_ref, u_ref, o_ref, kb_ref, vb_ref, *, blk, dh, hps, scale):
    hg, qi = pl.program_id(1), pl.program_id(2)

    @pl.when(qi == 0)
    def _():
        kb_ref[...] = k_ref[...].astype(BF16)
        vb_ref[...] = v_ref[...].astype(BF16)

    u = u_ref[...]
    heads = [slice(i * dh, (i + 1) * dh) for i in range(hps)]
    qs = [(q_ref[:, hd] * scale).astype(BF16) for hd in heads]
    biases = [bias_ref[hg * hps + i] for i in range(hps)]

    def visit(kb, state, valid):
        start = pl.multiple_of(kb * blk, blk)
        chains = [(qs[i], [kb_ref[pl.ds(start, blk), heads[i]]], [vb_ref[pl.ds(start, blk), heads[i]]],
                   biases[i], state[i][0], state[i][1], [valid]) for i in range(hps)]
        return _stick_chains(chains, u)

    row = lax.broadcasted_iota(jnp.int32, (blk, blk), 0)
    col = lax.broadcasted_iota(jnp.int32, (blk, blk), 1)
    state = [(jnp.zeros((blk, 1), F32), jnp.zeros((blk, dh), F32)) for _ in range(hps)]
    state = visit(qi, state, col < row)
    state = lax.fori_loop(0, qi, lambda it, st: visit(qi - 1 - it, st, None), state)
    for i in range(hps):
        o_ref[:, heads[i]] = state[i][1]


def _prompt_attention(q, k, v, bias, nh, dh):
    b, t, da = q.shape
    blk = _tile(t, 256)
    hps = min(PATTN_HEADS_PER_STEP, nh)
    assert nh % hps == 0
    u = _strict_lower(blk)
    qspec = pl.BlockSpec((None, blk, hps * dh), lambda n, h, i: (n, i, h))
    kvspec = pl.BlockSpec((None, t, hps * dh), lambda n, h, i: (n, 0, h))
    return pl.pallas_call(
        functools.partial(_pattn_kernel, blk=blk, dh=dh, hps=hps, scale=dh ** -0.5),
        grid=(b, nh // hps, t // blk),
        in_specs=[
            pl.BlockSpec(memory_space=pltpu.SMEM),
            qspec, kvspec, kvspec,
            pl.BlockSpec((blk, blk), lambda n, h, i: (0, 0)),
        ],
        out_specs=qspec,
        out_shape=jax.ShapeDtypeStruct((b, t, da), F32),
        scratch_shapes=[pltpu.VMEM((t, hps * dh), BF16), pltpu.VMEM((t, hps * dh), BF16)],
        compiler_params=_params(3),
        name="prompt_attn",
    )(bias, q, k, v, u)


SATTN_PAGES_PER_STEP = 8
SATTN_HEADS_PER_BLOCK = 16


def _sattn_kernel(pt_ref, q_ref, kn_ref, vn_ref, *rest, nh, dh, nt, page, ppb, hpb, scale):
    nhb = nh // hpb
    n_ops = ppb * nhb
    k_refs, v_refs = rest[:n_ops], rest[n_ops:2 * n_ops]
    bias_ref, u_ref, o_ref, qbd_ref, c_ref, acc_ref, kpad_ref, vpad_ref, kbuf_ref, vbuf_ref = rest[2 * n_ops:]
    p = pl.program_id(1)
    da = nh * dh
    nrows = nt * nh
    log_nh, log_dh = nh.bit_length() - 1, dh.bit_length() - 1
    r = lax.broadcasted_iota(jnp.int32, (nrows, da), 0)
    lane = lax.broadcasted_iota(jnp.int32, (nrows, da), 1)
    own_head = (lane >> log_dh) == (r & (nh - 1))

    @pl.when(p == 0)
    def _():
        qrep = jnp.concatenate([jnp.broadcast_to(q_ref[t:t + 1, :], (nh, da)) for t in range(nt)], axis=0)
        qbd_ref[...] = jnp.where(own_head, qrep * scale, 0.0).astype(BF16)
        kpad_ref[...] = jnp.zeros(kpad_ref.shape, F32)
        vpad_ref[...] = jnp.zeros(vpad_ref.shape, F32)
        kpad_ref[0:nt, :] = kn_ref[...]
        vpad_ref[0:nt, :] = vn_ref[...]
        rr = lax.broadcasted_iota(jnp.int32, (nrows, page), 0)
        cc = lax.broadcasted_iota(jnp.int32, (nrows, page), 1)
        valid = cc < (rr >> log_nh)
        (c, acc), = _stick_chains([(qbd_ref[...], [kpad_ref[...].astype(BF16)], [vpad_ref[...].astype(BF16)],
                                    bias_ref[:, 0:page], jnp.zeros((nrows, 1), F32), jnp.zeros((nrows, da), F32),
                                    [valid])], u_ref[0:page, 0:page])
        c_ref[...] = c
        acc_ref[...] = acc

    def regroup(src_refs, dst_ref):
        for j in range(ppb):
            for hb in range(nhb):
                by_head = jnp.swapaxes(src_refs[j * nhb + hb][...], 0, 1)
                for s in range(hpb):
                    head = hb * hpb + s
                    dst_ref[j * page:(j + 1) * page, head * dh:(head + 1) * dh] = by_head[s].astype(BF16)

    regroup(k_refs, kbuf_ref)
    regroup(v_refs, vbuf_ref)

    chunk = 2 * page
    order = list(reversed(range(ppb * page // chunk)))
    (c, acc), = _stick_chains([(qbd_ref[...], [kbuf_ref[ci * chunk:(ci + 1) * chunk, :] for ci in order],
                                [vbuf_ref[ci * chunk:(ci + 1) * chunk, :] for ci in order],
                                bias_ref[...], c_ref[...], acc_ref[...], [None] * len(order))], u_ref[...])
    c_ref[...] = c
    acc_ref[...] = acc

    @pl.when(p == pl.num_programs(1) - 1)
    def _():
        own = jnp.where(own_head, acc, 0.0)
        o_ref[...] = jnp.sum(own.reshape(nt, nh, da), axis=1)


def _sample_attention(q, k_new, v_new, k_pages, v_pages, page_table, bias, nh, dh):
    ns, nt, da = q.shape
    page = k_pages.shape[1]
    npg = page_table.shape[1]
    ppb = min(SATTN_PAGES_PER_STEP, npg)
    hpb = min(SATTN_HEADS_PER_BLOCK, nh)
    assert npg % ppb == 0 and ppb % 2 == 0 and nh % hpb == 0
    assert nh & (nh - 1) == 0 and dh & (dh - 1) == 0 and nt <= page
    nrows = nt * nh
    u = _strict_lower(2 * page)
    bias_rows = jnp.broadcast_to(jnp.tile(bias.astype(F32), nt)[:, None], (nrows, 2 * page))
    pt = page_table.reshape(-1).astype(jnp.int32)

    seq = pl.BlockSpec((None, nt, da), lambda n, p, pt: (n, 0, 0))

    def page_spec(j, hb):
        return pl.BlockSpec((None, page, hpb, dh),
                            lambda n, p, pt: (pt[n * npg + npg - ppb * (p + 1) + j], 0, hb, 0))

    page_specs = [page_spec(j, hb) for j in range(ppb) for hb in range(nh // hpb)]
    const = lambda shape: pl.BlockSpec(shape, lambda n, p, pt: (0, 0))
    grid_spec = pltpu.PrefetchScalarGridSpec(
        num_scalar_prefetch=1,
        grid=(ns, npg // ppb),
        in_specs=[seq, seq, seq] + page_specs + page_specs + [const((nrows, 2 * page)), const((2 * page, 2 * page))],
        out_specs=seq,
        scratch_shapes=[
            pltpu.VMEM((nrows, da), BF16),
            pltpu.VMEM((nrows, 1), F32),
            pltpu.VMEM((nrows, da), F32),
            pltpu.VMEM((page, da), F32),
            pltpu.VMEM((page, da), F32),
            pltpu.VMEM((ppb * page, da), BF16),
            pltpu.VMEM((ppb * page, da), BF16),
        ],
    )
    n_ops = len(page_specs)
    return pl.pallas_call(
        functools.partial(_sattn_kernel, nh=nh, dh=dh, nt=nt, page=page, ppb=ppb, hpb=hpb, scale=dh ** -0.5),
        grid_spec=grid_spec,
        out_shape=jax.ShapeDtypeStruct((ns, nt, da), F32),
        compiler_params=_params(2),
        name="sample_attn",
    )(pt, q, k_new, v_new, *([k_pages] * n_ops), *([v_pages] * n_ops), bias_rows, u)


def _shift_rows(x, s):
    row = lax.broadcasted_iota(jnp.int32, x.shape, 0)
    return jnp.where(row >= s, pltpu.roll(x, s, 0), 0.0)


def _conv_rows(u, w_ref):
    return w_ref[0:1, :] * _shift_rows(u, 2) + w_ref[1:2, :] * _shift_rows(u, 1) + w_ref[2:3, :] * u


def _conv_time_major(us, prev_ref, w_ref):
    seq = [prev_ref[0], prev_ref[1]] + us
    ys = [w_ref[0:1, :] * seq[t] + w_ref[1:2, :] * seq[t + 1] + w_ref[2:3, :] * seq[t + 2] for t in range(len(us))]
    return ys, seq[-2:]


def _convgate_p_kernel(b_ref, c_ref, u_ref, w_ref, short_ref, st_ref):
    t = c_ref.shape[0]
    st_ref[...] = c_ref[t - 2:t, :] * u_ref[t - 2:t, :]
    short_ref[...] = b_ref[...] * _conv_rows(c_ref[...] * u_ref[...], w_ref)


def _convgate_prompt(pg, conv_w, n_seq, t, dc):
    tc = _tile(dc, 256)
    nj = dc // tc
    return pl.pallas_call(
        _convgate_p_kernel,
        grid=(n_seq, nj),
        in_specs=[
            pl.BlockSpec((t, tc), lambda n, j: (n, j)),
            pl.BlockSpec((t, tc), lambda n, j: (n, j + nj)),
            pl.BlockSpec((t, tc), lambda n, j: (n, j + 2 * nj)),
            pl.BlockSpec((CONV_WIDTH, tc), lambda n, j: (0, j)),
        ],
        out_specs=[pl.BlockSpec((t, tc), lambda n, j: (n, j)), pl.BlockSpec((None, 2, tc), lambda n, j: (n, 0, j))],
        out_shape=[jax.ShapeDtypeStruct((n_seq * t, dc), F32), jax.ShapeDtypeStruct((n_seq, 2, dc), F32)],
        compiler_params=_params(2),
        name="convgate_prompt",
    )(pg, pg, pg, conv_w)


def _fc1_gate_kernel(h_ref, wa_ref, wb_ref, cw_ref, *rest, prev_steps):
    h = h_ref[...]
    a = jnp.dot(h, wa_ref[...], preferred_element_type=F32)
    b = jnp.dot(h, wb_ref[...], preferred_element_type=F32)
    if prev_steps == 0:
        g_ref, st_ref = rest
        t = a.shape[0]
        st_ref[...] = a[t - 8:t, :]
        g_ref[...] = (_silu(_conv_rows(a, cw_ref)) * b).astype(g_ref.dtype)
    else:
        prev_ref, g_ref, st_ref = rest
        ns = prev_ref.shape[1]
        ys, last = _conv_time_major([a[t * ns:(t + 1) * ns, :] for t in range(prev_steps)], prev_ref, cw_ref)
        for t in range(prev_steps):
            g_ref[t * ns:(t + 1) * ns, :] = (_silu(ys[t]) * b[t * ns:(t + 1) * ns, :]).astype(g_ref.dtype)
        st_ref[0] = last[0]
        st_ref[1] = last[1]


def _fc1_gate(h, w_fc1, conv_w, dff, *, tm, prev=None):
    m, d = h.shape
    tn = _tile(dff, 256)
    nj = dff // tn
    in_specs = [
        pl.BlockSpec((tm, d), lambda i, j: (i, 0), pipeline_mode=pl.Buffered(1)),
        pl.BlockSpec((d, tn), lambda i, j: (0, j)),
        pl.BlockSpec((d, tn), lambda i, j: (0, j + nj)),
        pl.BlockSpec((CONV_WIDTH, tn), lambda i, j: (0, j)),
    ]
    args = [h, w_fc1, w_fc1, conv_w]
    if prev is None:
        prev_steps = 0
        st_spec = pl.BlockSpec((None, 8, tn), lambda i, j: (i, 0, j))
        st_shape = jax.ShapeDtypeStruct((m // tm, 8, dff), F32)
    else:
        ns = prev.shape[1]
        prev_steps = m // ns
        assert tm == m
        in_specs.append(pl.BlockSpec((2, ns, tn), lambda i, j: (0, 0, j)))
        args.append(prev)
        st_spec = pl.BlockSpec((2, ns, tn), lambda i, j: (0, 0, j))
        st_shape = jax.ShapeDtypeStruct((2, ns, dff), F32)
    return pl.pallas_call(
        functools.partial(_fc1_gate_kernel, prev_steps=prev_steps),
        grid=(m // tm, nj),
        in_specs=in_specs,
        out_specs=[pl.BlockSpec((tm, tn), lambda i, j: (i, j)), st_spec],
        out_shape=[jax.ShapeDtypeStruct((m, dff), BF16), st_shape],
        compiler_params=_params(2),
        name="fc1_gate",
    )(*args)


def _convgate_s_kernel(b_ref, c_ref, u_ref, prev_ref, w_ref, short_ref, st_ref, *, nt):
    ns = prev_ref.shape[1]
    cu = [c_ref[t * ns:(t + 1) * ns, :] * u_ref[t * ns:(t + 1) * ns, :] for t in range(nt)]
    ys, last = _conv_time_major(cu, prev_ref, w_ref)
    for t in range(nt):
        short_ref[t * ns:(t + 1) * ns, :] = b_ref[t * ns:(t + 1) * ns, :] * ys[t]
    st_ref[0] = last[0]
    st_ref[1] = last[1]


def _convgate_sample(pg, prev, conv_w, nt, ns, dc):
    tc = _tile(dc, 512)
    nj = dc // tc
    rows = nt * ns
    return pl.pallas_call(
        functools.partial(_convgate_s_kernel, nt=nt),
        grid=(nj,),
        in_specs=[
            pl.BlockSpec((rows, tc), lambda j: (0, j)),
            pl.BlockSpec((rows, tc), lambda j: (0, j + nj)),
            pl.BlockSpec((rows, tc), lambda j: (0, j + 2 * nj)),
            pl.BlockSpec((2, ns, tc), lambda j: (0, 0, j)),
            pl.BlockSpec((CONV_WIDTH, tc), lambda j: (0, j)),
        ],
        out_specs=[pl.BlockSpec((rows, tc), lambda j: (0, j)), pl.BlockSpec((2, ns, tc), lambda j: (0, 0, j))],
        out_shape=[jax.ShapeDtypeStruct((rows, dc), F32), jax.ShapeDtypeStruct((2, ns, dc), F32)],
        compiler_params=_params(1),
        name="convgate_sample",
    )(pg, pg, pg, prev, conv_w)


MOD_SH1, MOD_SC1, MOD_GT1, MOD_SH2, MOD_SC2, MOD_GT2 = range(6)


def _layer(x, rows, attend, convgate, ffn_gate, wts, da, dc, dff):
    h = _prenorm(rows, x, wts["g_pre_mix"], MOD_SH1, MOD_SC1)
    q = _matmul(h, wts["w_in"], col0=0, ncols=da, name="proj_q")
    k = _matmul(h, wts["w_in"], col0=da, ncols=da, name="proj_k")
    v = _matmul(h, wts["w_in"], col0=2 * da, ncols=da, name="proj_v")
    pg = _matmul(h, wts["w_in"], col0=3 * da, ncols=3 * dc, name="proj_conv")
    att = attend(q, k, v)
    short, conv_state = convgate(pg)
    merged = _merge_norm(rows, att, short, wts["g_attn_out"], wts["g_conv_out"])
    o = _matmul(merged, wts["w_o"], name="out_proj")
    x1, h2 = _resid_prenorm(rows, x, o, wts["g_post_mix"], MOD_GT1, wts["g_pre_ffn"], MOD_SH2, MOD_SC2)
    g, ffn_state = ffn_gate(h2)
    f = _matmul(g, wts["w_fc2"], name="fc2")
    y = _resid(rows, x1, f, wts["g_post_ffn"], MOD_GT2)
    return y, k, v, conv_state, ffn_state


def kernel(x_prompt, x_sample, c_prompt, c_sample, cache_k, cache_v, state_conv, state_ffn_conv, page_table, w_ada, b_ada, g_pre_mix, g_post_mix, w_in, sb_bias, conv_w, g_attn_out, g_conv_out, w_o, g_pre_ffn, g_post_ffn, w_fc1, ffn_conv_w, w_fc2):
    b, t, d = x_prompt.shape
    ns, nt, _ = x_sample.shape
    depth, n_phys, page, nh, dh = cache_k.shape
    assert depth == 1 and conv_w.shape[1] == CONV_WIDTH and ffn_conv_w.shape[1] == CONV_WIDTH
    da, dc, dff = nh * dh, conv_w.shape[-1], ffn_conv_w.shape[-1]

    wts = {
        "w_in": w_in.reshape(d, -1).astype(BF16),
        "w_o": w_o.reshape(da + dc, d).astype(BF16),
        "w_fc1": w_fc1.reshape(d, 2 * dff).astype(BF16),
        "w_fc2": w_fc2.reshape(dff, d).astype(BF16),
        "g_pre_mix": g_pre_mix.reshape(1, d), "g_post_mix": g_post_mix.reshape(1, d),
        "g_attn_out": g_attn_out.reshape(1, da), "g_conv_out": g_conv_out.reshape(1, dc),
        "g_pre_ffn": g_pre_ffn.reshape(1, d), "g_post_ffn": g_post_ffn.reshape(1, d),
    }
    cw, fcw = conv_w.reshape(CONV_WIDTH, dc), ffn_conv_w.reshape(CONV_WIDTH, dff)
    bias = sb_bias.reshape(nh)

    n_c = ns + b
    pad = -n_c % 16
    c_all = jnp.concatenate([c_sample, c_prompt, jnp.zeros((pad, d), F32)], axis=0)
    mod = _ada(c_all, w_ada.reshape(d, 6 * d), b_ada.reshape(1, 6 * d))
    mod_s = mod[:ns].reshape(1, ns, 6 * d)
    mod_p = mod[ns:n_c].reshape(b, 1, 6 * d)

    tm_p = _tile(t, 256)
    rows_p = _Rows(b * t, tm_p, t // tm_p, mod_p)

    def attend_p(q, k, v):
        r3 = lambda a: a.reshape(b, t, da)
        return _prompt_attention(r3(q), r3(k), r3(v), bias, nh, dh).reshape(b * t, da)

    yp, kp, vp, cp, fp = _layer(
        x_prompt.reshape(b * t, d), rows_p, attend_p,
        lambda pg: _convgate_prompt(pg, cw, b, t, dc),
        lambda h2: _fc1_gate(h2, wts["w_fc1"], fcw, dff, tm=t),
        wts, da, dc, dff)

    rows_s = _Rows(nt * ns, ns, nt, mod_s)
    to_seq_major = lambda a: a.reshape(nt, ns, -1).transpose(1, 0, 2)
    to_time_major = lambda a: a.transpose(1, 0, 2).reshape(nt * ns, -1)
    kv_seq = {}

    def attend_s(q, k, v):
        kv_seq["k"], kv_seq["v"] = to_seq_major(k), to_seq_major(v)
        att = _sample_attention(to_seq_major(q), kv_seq["k"], kv_seq["v"],
                                cache_k.reshape(n_phys, page, nh, dh), cache_v.reshape(n_phys, page, nh, dh),
                                page_table, bias, nh, dh)
        return to_time_major(att)

    prev_conv = state_conv.reshape(ns, CONV_WIDTH - 1, dc).transpose(1, 0, 2)
    prev_ffn = state_ffn_conv.reshape(ns, CONV_WIDTH - 1, dff).transpose(1, 0, 2)
    ys, _, _, cs, fs = _layer(
        to_time_major(x_sample), rows_s, attend_s,
        lambda pg: _convgate_sample(pg, prev_conv, cw, nt, ns, dc),
        lambda h2: _fc1_gate(h2, wts["w_fc1"], fcw, dff, tm=nt * ns, prev=prev_ffn),
        wts, da, dc, dff)

    return (
        yp.reshape(b, t, d),
        to_seq_major(ys),
        kp.reshape(1, b, t, nh, dh),
        vp.reshape(1, b, t, nh, dh),
        cp.reshape(1, b, CONV_WIDTH - 1, dc),
        fp[:, 8 - (CONV_WIDTH - 1):, :].reshape(1, b, CONV_WIDTH - 1, dff),
        kv_seq["k"].reshape(1, ns, nt, nh, dh),
        kv_seq["v"].reshape(1, ns, nt, nh, dh),
        cs.transpose(1, 0, 2).reshape(1, ns, CONV_WIDTH - 1, dc),
        fs.transpose(1, 0, 2).reshape(1, ns, CONV_WIDTH - 1, dff),
    )
```

```python
import functools

import jax
import jax.numpy as jnp
from jax import lax
from jax.experimental import pallas as pl
from jax.experimental.pallas import tpu as pltpu

F32 = jnp.float32
BF16 = jnp.bfloat16
EPS = 1e-6
CONV_WIDTH = 3
V7X_VMEM_LIMIT_BYTES = 56 * 1024 * 1024
LANE = 128
NT_DIMS = (((1,), (1,)), ((), ()))


def _params(n_axes, vmem=V7X_VMEM_LIMIT_BYTES):
    return pltpu.CompilerParams(dimension_semantics=("arbitrary",) * n_axes, vmem_limit_bytes=vmem)


def _tile(dim, pref):
    if dim <= pref:
        return dim
    t = (pref // LANE) * LANE
    while dim % t:
        t -= LANE
    return t


def _rms(x):
    return x * lax.rsqrt(jnp.mean(x * x, axis=-1, keepdims=True) + EPS)


def _silu(x):
    return x * jax.nn.sigmoid(x)


def _ada_kernel(c_ref, w_ref, b_ref, o_ref, a_ref):
    @pl.when(pl.program_id(0) == 0)
    def _():
        a_ref[...] = _silu(c_ref[...]).astype(BF16)

    o_ref[...] = jnp.dot(a_ref[...], w_ref[...].astype(BF16), preferred_element_type=F32) + b_ref[...]


def _ada(c_all, w, b):
    rows, d = c_all.shape
    n = w.shape[1]
    tn = _tile(n, 512)
    return pl.pallas_call(
        _ada_kernel,
        grid=(n // tn,),
        in_specs=[
            pl.BlockSpec((rows, d), lambda j: (0, 0)),
            pl.BlockSpec((d, tn), lambda j: (0, j)),
            pl.BlockSpec((1, tn), lambda j: (0, j)),
        ],
        out_specs=pl.BlockSpec((rows, tn), lambda j: (0, j)),
        out_shape=jax.ShapeDtypeStruct((rows, n), F32),
        scratch_shapes=[pltpu.VMEM((rows, d), BF16)],
        compiler_params=_params(1),
        name="ada_mod",
    )(c_all, w, b)


class _Rows:
    def __init__(self, n_rows, tm, tiles_per_group, mod):
        self.n_rows, self.tm, self.tpg, self.mod = n_rows, tm, tiles_per_group, mod
        self.grid = (n_rows // tm,)

    def row(self, width, col=0):
        return pl.BlockSpec((self.tm, width), lambda i: (i, col))

    def vec(self, width):
        return pl.BlockSpec((1, width), lambda i: (0, 0))

    def modv(self, chunk, width):
        tpg = self.tpg
        return pl.BlockSpec((None, self.mod.shape[1], width), lambda i: (i // tpg, 0, chunk))


def _prenorm_kernel(x_ref, g_ref, sh_ref, sc_ref, h_ref):
    y = _rms(x_ref[...]) * g_ref[...]
    h_ref[...] = (y * (1.0 + sc_ref[...]) + sh_ref[...]).astype(h_ref.dtype)


def _prenorm(rows, x, g, sh_chunk, sc_chunk):
    d = x.shape[1]
    return pl.pallas_call(
        _prenorm_kernel,
        grid=rows.grid,
        in_specs=[rows.row(d), rows.vec(d), rows.modv(sh_chunk, d), rows.modv(sc_chunk, d)],
        out_specs=rows.row(d),
        out_shape=jax.ShapeDtypeStruct(x.shape, BF16),
        compiler_params=_params(1),
        name="prenorm",
    )(x, g, rows.mod, rows.mod)


def _merge_norm_kernel(att_ref, short_ref, ga_ref, gc_ref, o_ref):
    da = att_ref.shape[1]
    o_ref[:, :da] = (_rms(att_ref[...]) * ga_ref[...]).astype(o_ref.dtype)
    o_ref[:, da:] = (_rms(short_ref[...]) * gc_ref[...]).astype(o_ref.dtype)


def _merge_norm(rows, att, short, ga, gc):
    da, dc = att.shape[1], short.shape[1]
    return pl.pallas_call(
        _merge_norm_kernel,
        grid=rows.grid,
        in_specs=[rows.row(da), rows.row(dc), rows.vec(da), rows.vec(dc)],
        out_specs=rows.row(da + dc),
        out_shape=jax.ShapeDtypeStruct((att.shape[0], da + dc), BF16),
        compiler_params=_params(1),
        name="merge_norm",
    )(att, short, ga, gc)


def _resid_prenorm_kernel(x_ref, o_ref, gpost_ref, gt_ref, gpre_ref, sh_ref, sc_ref, x1_ref, h_ref):
    x1 = x_ref[...] + gt_ref[...] * (_rms(o_ref[...]) * gpost_ref[...])
    x1_ref[...] = x1
    y = _rms(x1) * gpre_ref[...]
    h_ref[...] = (y * (1.0 + sc_ref[...]) + sh_ref[...]).astype(h_ref.dtype)


def _resid_prenorm(rows, x, o, gpost, gt_chunk, gpre, sh_chunk, sc_chunk):
    d = x.shape[1]
    return pl.pallas_call(
        _resid_prenorm_kernel,
        grid=rows.grid,
        in_specs=[rows.row(d), rows.row(d), rows.vec(d), rows.modv(gt_chunk, d), rows.vec(d),
                  rows.modv(sh_chunk, d), rows.modv(sc_chunk, d)],
        out_specs=[rows.row(d), rows.row(d)],
        out_shape=[jax.ShapeDtypeStruct(x.shape, F32), jax.ShapeDtypeStruct(x.shape, BF16)],
        compiler_params=_params(1),
        name="resid_prenorm",
    )(x, o, gpost, rows.mod, gpre, rows.mod, rows.mod)


def _resid_kernel(x_ref, f_ref, gpost_ref, gt_ref, y_ref):
    y_ref[...] = x_ref[...] + gt_ref[...] * (_rms(f_ref[...]) * gpost_ref[...])


def _resid(rows, x, f, gpost, gt_chunk):
    d = x.shape[1]
    return pl.pallas_call(
        _resid_kernel,
        grid=rows.grid,
        in_specs=[rows.row(d), rows.row(d), rows.vec(d), rows.modv(gt_chunk, d)],
        out_specs=rows.row(d),
        out_shape=jax.ShapeDtypeStruct(x.shape, F32),
        compiler_params=_params(1),
        name="resid",
    )(x, f, gpost, rows.mod)


def _mm_kernel(a_ref, w_ref, *o_refs, nk, scale):
    acc = jnp.dot(a_ref[...], w_ref[...].astype(BF16), preferred_element_type=F32)
    if nk == 1:
        if scale is not None:
            acc = acc * scale
        for o_ref in o_refs:
            o_ref[...] = acc.astype(o_ref.dtype)
    else:
        o_ref, = o_refs
        k = pl.program_id(2)

        @pl.when(k == 0)
        def _():
            o_ref[...] = acc

        @pl.when(k != 0)
        def _():
            o_ref[...] += acc


def _matmul(a, w, *, col0=0, ncols=None, out_dtypes=(F32,), scale=None, tm_pref=1024, tn_pref=512, tk_pref=5632,
            name="matmul"):
    m, kdim = a.shape
    ncols = w.shape[1] if ncols is None else ncols
    tm, tn, tk = _tile(m, tm_pref), _tile(ncols, tn_pref), _tile(kdim, tk_pref)
    assert col0 % tn == 0
    cb, nk = col0 // tn, kdim // tk
    assert nk == 1 or (tuple(out_dtypes) == (F32,) and scale is None)
    outs = pl.pallas_call(
        functools.partial(_mm_kernel, nk=nk, scale=scale),
        grid=(m // tm, ncols // tn, nk),
        in_specs=[
            pl.BlockSpec((tm, tk), lambda i, j, k: (i, k)),
            pl.BlockSpec((tk, tn), lambda i, j, k: (k, j + cb)),
        ],
        out_specs=[pl.BlockSpec((tm, tn), lambda i, j, k: (i, j)) for _ in out_dtypes],
        out_shape=[jax.ShapeDtypeStruct((m, ncols), dt) for dt in out_dtypes],
        compiler_params=_params(3),
        name=name,
    )(a, w)
    return outs[0] if len(out_dtypes) == 1 else outs


LOG2E = 1.4426950408889634


def _stick_chains(chains, u2):
    items = [(ci, bi) for ci, ch in enumerate(chains) for bi in range(len(ch[1]))]
    y, sp, drop, cum, a = {}, {}, {}, {}, {}
    for ci, bi in items:
        q, ks, _, bias, _, _, _ = chains[ci]
        y[ci, bi] = lax.dot_general(q, ks[bi], NT_DIMS, preferred_element_type=F32) + bias
    for it in items:
        sp[it] = jnp.maximum(y[it], 0.0) + jnp.log2(1.0 + jnp.exp2(-jnp.abs(y[it])))
        valid = chains[it[0]][6][it[1]]
        drop[it] = sp[it] if valid is None else jnp.where(valid, sp[it], 0.0)
    for it in items:
        hi = drop[it].astype(BF16)
        lo = (drop[it] - hi.astype(F32)).astype(BF16)
        cum[it] = jnp.dot(jnp.concatenate([hi, lo], axis=1), u2, preferred_element_type=F32)
    ds = [ch[4] for ch in chains]
    for it in items:
        ci = it[0]
        valid = chains[ci][6][it[1]]
        e = jnp.exp2((y[it] - sp[it]) - (cum[it] + ds[ci]))
        a[it] = (e if valid is None else jnp.where(valid, e, 0.0)).astype(BF16)
        ds[ci] = ds[ci] + jnp.sum(drop[it], axis=-1, keepdims=True)
    accs = [ch[5] for ch in chains]
    for ci, bi in items:
        accs[ci] = accs[ci] + jnp.dot(a[ci, bi], chains[ci][2][bi], preferred_element_type=F32)
    return list(zip(ds, accs))


def _strict_lower_twice(n):
    r = lax.broadcasted_iota(jnp.int32, (n, n), 0)
    c = lax.broadcasted_iota(jnp.int32, (n, n), 1)
    lower = (r > c).astype(BF16)
    return jnp.concatenate([lower, lower], axis=0)


PATTN_HEADS_PER_STEP = 8


def _pattn_kernel(bias_ref, q_ref, k_ref, v_ref, u_ref, o_ref, *, blk, dh, hps):
    hg, qi = pl.program_id(1), pl.program_id(2)
    u2 = u_ref[...]
    heads = [slice(i * dh, (i + 1) * dh) for i in range(hps)]
    qs = [q_ref[:, hd] for hd in heads]
    biases = [LOG2E * bias_ref[hg * hps + i] for i in range(hps)]

    def visit(kb, state, valid):
        start = pl.multiple_of(kb * blk, blk)
        chains = [(qs[i], [k_ref[pl.ds(start, blk), heads[i]]], [v_ref[pl.ds(start, blk), heads[i]]],
                   biases[i], state[i][0], state[i][1], [valid]) for i in range(hps)]
        return _stick_chains(chains, u2)

    row = lax.broadcasted_iota(jnp.int32, (blk, blk), 0)
    col = lax.broadcasted_iota(jnp.int32, (blk, blk), 1)
    state = [(jnp.zeros((blk, 1), F32), jnp.zeros((blk, dh), F32)) for _ in range(hps)]
    state = visit(qi, state, col < row)
    state = lax.fori_loop(0, qi, lambda it, st: visit(qi - 1 - it, st, None), state)
    for i in range(hps):
        o_ref[:, heads[i]] = state[i][1]


def _prompt_attention(q, k, v, bias, nh, dh):
    b, t, da = q.shape
    blk = _tile(t, 256)
    hps = min(PATTN_HEADS_PER_STEP, nh)
    assert nh % hps == 0
    u2 = _strict_lower_twice(blk)
    qspec = pl.BlockSpec((None, blk, hps * dh), lambda n, h, i: (n, i, h))
    kvspec = pl.BlockSpec((None, t, hps * dh), lambda n, h, i: (n, 0, h))
    return pl.pallas_call(
        functools.partial(_pattn_kernel, blk=blk, dh=dh, hps=hps),
        grid=(b, nh // hps, t // blk),
        in_specs=[
            pl.BlockSpec(memory_space=pltpu.SMEM),
            qspec, kvspec, kvspec,
            pl.BlockSpec((2 * blk, blk), lambda n, h, i: (0, 0)),
        ],
        out_specs=qspec,
        out_shape=jax.ShapeDtypeStruct((b, t, da), F32),
        compiler_params=_params(3),
        name="prompt_attn",
    )(bias, q, k, v, u2)


SATTN_PAGES_PER_STEP = 8
SATTN_HEADS_PER_BLOCK = 16


def _sattn_kernel(pt_ref, q_ref, kn_ref, vn_ref, *rest, nh, dh, nt, page, ppb, hpb, scale):
    nhb = nh // hpb
    n_ops = ppb * nhb
    k_refs, v_refs = rest[:n_ops], rest[n_ops:2 * n_ops]
    bias_ref, u_ref, o_ref, qbd_ref, d_ref, acc_ref, kpad_ref, vpad_ref, kbuf_ref, vbuf_ref = rest[2 * n_ops:]
    p = pl.program_id(1)
    da = nh * dh
    nrows = nt * nh
    log_nh, log_dh = nh.bit_length() - 1, dh.bit_length() - 1
    r = lax.broadcasted_iota(jnp.int32, (nrows, da), 0)
    lane = lax.broadcasted_iota(jnp.int32, (nrows, da), 1)
    own_head = (lane >> log_dh) == (r & (nh - 1))

    @pl.when(p == 0)
    def _():
        qrep = jnp.concatenate([jnp.broadcast_to(q_ref[t:t + 1, :], (nh, da)) for t in range(nt)], axis=0)
        qbd_ref[...] = jnp.where(own_head, qrep * (LOG2E * scale), 0.0).astype(BF16)
        kpad_ref[...] = jnp.zeros(kpad_ref.shape, F32)
        vpad_ref[...] = jnp.zeros(vpad_ref.shape, F32)
        kpad_ref[0:nt, :] = kn_ref[...]
        vpad_ref[0:nt, :] = vn_ref[...]
        rr = lax.broadcasted_iota(jnp.int32, (nrows, page), 0)
        cc = lax.broadcasted_iota(jnp.int32, (nrows, page), 1)
        valid = cc < (rr >> log_nh)
        u2_new = jnp.concatenate([u_ref[0:page, 0:page], u_ref[2 * page:3 * page, 0:page]], axis=0)
        (d, acc), = _stick_chains([(qbd_ref[...], [kpad_ref[...].astype(BF16)], [vpad_ref[...].astype(BF16)],
                                    bias_ref[:, 0:page], jnp.zeros((nrows, 1), F32), jnp.zeros((nrows, da), F32),
                                    [valid])], u2_new)
        d_ref[...] = d
        acc_ref[...] = acc

    def regroup(src_refs, dst_ref):
        for j in range(ppb):
            for hb in range(nhb):
                by_head = jnp.swapaxes(src_refs[j * nhb + hb][...].astype(BF16), 0, 1)
                for s in range(hpb):
                    head = hb * hpb + s
                    dst_ref[j * page:(j + 1) * page, head * dh:(head + 1) * dh] = by_head[s]

    regroup(k_refs, kbuf_ref)
    regroup(v_refs, vbuf_ref)

    chunk = 2 * page
    order = list(reversed(range(ppb * page // chunk)))
    (d, acc), = _stick_chains([(qbd_ref[...], [kbuf_ref[ci * chunk:(ci + 1) * chunk, :] for ci in order],
                                [vbuf_ref[ci * chunk:(ci + 1) * chunk, :] for ci in order],
                                bias_ref[...], d_ref[...], acc_ref[...], [None] * len(order))], u_ref[...])
    d_ref[...] = d
    acc_ref[...] = acc

    @pl.when(p == pl.num_programs(1) - 1)
    def _():
        own = jnp.where(own_head, acc, 0.0)
        o_ref[...] = jnp.sum(own.reshape(nt, nh, da), axis=1)


def _sample_attention(q, k_new, v_new, k_pages, v_pages, page_table, bias, nh, dh):
    ns, nt, da = q.shape
    page = k_pages.shape[1]
    npg = page_table.shape[1]
    ppb = min(SATTN_PAGES_PER_STEP, npg)
    hpb = min(SATTN_HEADS_PER_BLOCK, nh)
    assert npg % ppb == 0 and ppb % 2 == 0 and nh % hpb == 0
    assert nh & (nh - 1) == 0 and dh & (dh - 1) == 0 and nt <= page
    nrows = nt * nh
    u2 = _strict_lower_twice(2 * page)
    bias_rows = jnp.broadcast_to(jnp.tile(LOG2E * bias.astype(F32), nt)[:, None], (nrows, 2 * page))
    pt = page_table.reshape(-1).astype(jnp.int32)

    seq = pl.BlockSpec((None, nt, da), lambda n, p, pt: (n, 0, 0))

    def page_spec(j, hb):
        return pl.BlockSpec((None, page, hpb, dh),
                            lambda n, p, pt: (pt[n * npg + npg - ppb * (p + 1) + j], 0, hb, 0))

    page_specs = [page_spec(j, hb) for j in range(ppb) for hb in range(nh // hpb)]
    const = lambda shape: pl.BlockSpec(shape, lambda n, p, pt: (0, 0))
    grid_spec = pltpu.PrefetchScalarGridSpec(
        num_scalar_prefetch=1,
        grid=(ns, npg // ppb),
        in_specs=[seq, seq, seq] + page_specs + page_specs + [const((nrows, 2 * page)), const((4 * page, 2 * page))],
        out_specs=seq,
        scratch_shapes=[
            pltpu.VMEM((nrows, da), BF16),
            pltpu.VMEM((nrows, 1), F32),
            pltpu.VMEM((nrows, da), F32),
            pltpu.VMEM((page, da), F32),
            pltpu.VMEM((page, da), F32),
            pltpu.VMEM((ppb * page, da), BF16),
            pltpu.VMEM((ppb * page, da), BF16),
        ],
    )
    n_ops = len(page_specs)
    return pl.pallas_call(
        functools.partial(_sattn_kernel, nh=nh, dh=dh, nt=nt, page=page, ppb=ppb, hpb=hpb, scale=dh ** -0.5),
        grid_spec=grid_spec,
        out_shape=jax.ShapeDtypeStruct((ns, nt, da), F32),
        compiler_params=_params(2),
        name="sample_attn",
    )(pt, q, k_new, v_new, *([k_pages] * n_ops), *([v_pages] * n_ops), bias_rows, u2)


def _shift_rows(x, s):
    row = lax.broadcasted_iota(jnp.int32, x.shape, 0)
    return jnp.where(row >= s, pltpu.roll(x, s, 0), 0.0)


def _conv_rows(u, w_ref):
    return w_ref[0:1, :] * _shift_rows(u, 2) + w_ref[1:2, :] * _shift_rows(u, 1) + w_ref[2:3, :] * u


def _conv_time_major(us, prev_ref, w_ref):
    seq = [prev_ref[0], prev_ref[1]] + us
    ys = [w_ref[0:1, :] * seq[t] + w_ref[1:2, :] * seq[t + 1] + w_ref[2:3, :] * seq[t + 2] for t in range(len(us))]
    return ys, seq[-2:]


def _convgate_p_kernel(b_ref, c_ref, u_ref, w_ref, short_ref, st_ref):
    t = c_ref.shape[0]
    st_ref[...] = c_ref[t - 2:t, :] * u_ref[t - 2:t, :]
    short_ref[...] = b_ref[...] * _conv_rows(c_ref[...] * u_ref[...], w_ref)


def _convgate_prompt(pg, conv_w, n_seq, t, dc):
    tc = _tile(dc, 256)
    nj = dc // tc
    return pl.pallas_call(
        _convgate_p_kernel,
        grid=(n_seq, nj),
        in_specs=[
            pl.BlockSpec((t, tc), lambda n, j: (n, j)),
            pl.BlockSpec((t, tc), lambda n, j: (n, j + nj)),
            pl.BlockSpec((t, tc), lambda n, j: (n, j + 2 * nj)),
            pl.BlockSpec((CONV_WIDTH, tc), lambda n, j: (0, j)),
        ],
        out_specs=[pl.BlockSpec((t, tc), lambda n, j: (n, j)), pl.BlockSpec((None, 2, tc), lambda n, j: (n, 0, j))],
        out_shape=[jax.ShapeDtypeStruct((n_seq * t, dc), F32), jax.ShapeDtypeStruct((n_seq, 2, dc), F32)],
        compiler_params=_params(2),
        name="convgate_prompt",
    )(pg, pg, pg, conv_w)


def _fc1_gate_kernel(h_ref, wa_ref, wb_ref, cw_ref, *rest, prev_steps):
    h = h_ref[...]
    a = jnp.dot(h, wa_ref[...].astype(BF16), preferred_element_type=F32)
    b = jnp.dot(h, wb_ref[...].astype(BF16), preferred_element_type=F32)
    if prev_steps == 0:
        g_ref, st_ref = rest
        t = a.shape[0]
        st_ref[...] = a[t - 8:t, :]
        g_ref[...] = (_silu(_conv_rows(a, cw_ref)) * b).astype(g_ref.dtype)
    else:
        prev_ref, g_ref, st_ref = rest
        ns = prev_ref.shape[1]
        ys, last = _conv_time_major([a[t * ns:(t + 1) * ns, :] for t in range(prev_steps)], prev_ref, cw_ref)
        for t in range(prev_steps):
            g_ref[t * ns:(t + 1) * ns, :] = (_silu(ys[t]) * b[t * ns:(t + 1) * ns, :]).astype(g_ref.dtype)
        st_ref[0] = last[0]
        st_ref[1] = last[1]


def _fc1_gate(h, w_fc1, conv_w, dff, *, tm, prev=None):
    m, d = h.shape
    tn = _tile(dff, 256)
    nj = dff // tn
    in_specs = [
        pl.BlockSpec((tm, d), lambda i, j: (i, 0), pipeline_mode=pl.Buffered(1)),
        pl.BlockSpec((d, tn), lambda i, j: (0, j)),
        pl.BlockSpec((d, tn), lambda i, j: (0, j + nj)),
        pl.BlockSpec((CONV_WIDTH, tn), lambda i, j: (0, j)),
    ]
    args = [h, w_fc1, w_fc1, conv_w]
    if prev is None:
        prev_steps = 0
        st_spec = pl.BlockSpec((None, 8, tn), lambda i, j: (i, 0, j))
        st_shape = jax.ShapeDtypeStruct((m // tm, 8, dff), F32)
    else:
        ns = prev.shape[1]
        prev_steps = m // ns
        assert tm == m
        in_specs.append(pl.BlockSpec((2, ns, tn), lambda i, j: (0, 0, j)))
        args.append(prev)
        st_spec = pl.BlockSpec((2, ns, tn), lambda i, j: (0, 0, j))
        st_shape = jax.ShapeDtypeStruct((2, ns, dff), F32)
    return pl.pallas_call(
        functools.partial(_fc1_gate_kernel, prev_steps=prev_steps),
        grid=(m // tm, nj),
        in_specs=in_specs,
        out_specs=[pl.BlockSpec((tm, tn), lambda i, j: (i, j)), st_spec],
        out_shape=[jax.ShapeDtypeStruct((m, dff), BF16), st_shape],
        compiler_params=_params(2),
        name="fc1_gate",
    )(*args)


def _convgate_s_kernel(b_ref, c_ref, u_ref, prev_ref, w_ref, short_ref, st_ref, *, nt):
    ns = prev_ref.shape[1]
    cu = [c_ref[t * ns:(t + 1) * ns, :] * u_ref[t * ns:(t + 1) * ns, :] for t in range(nt)]
    ys, last = _conv_time_major(cu, prev_ref, w_ref)
    for t in range(nt):
        short_ref[t * ns:(t + 1) * ns, :] = b_ref[t * ns:(t + 1) * ns, :] * ys[t]
    st_ref[0] = last[0]
    st_ref[1] = last[1]


def _convgate_sample(pg, prev, conv_w, nt, ns, dc):
    tc = _tile(dc, 512)
    nj = dc // tc
    rows = nt * ns
    return pl.pallas_call(
        functools.partial(_convgate_s_kernel, nt=nt),
        grid=(nj,),
        in_specs=[
            pl.BlockSpec((rows, tc), lambda j: (0, j)),
            pl.BlockSpec((rows, tc), lambda j: (0, j + nj)),
            pl.BlockSpec((rows, tc), lambda j: (0, j + 2 * nj)),
            pl.BlockSpec((2, ns, tc), lambda j: (0, 0, j)),
            pl.BlockSpec((CONV_WIDTH, tc), lambda j: (0, j)),
        ],
        out_specs=[pl.BlockSpec((rows, tc), lambda j: (0, j)), pl.BlockSpec((2, ns, tc), lambda j: (0, 0, j))],
        out_shape=[jax.ShapeDtypeStruct((rows, dc), F32), jax.ShapeDtypeStruct((2, ns, dc), F32)],
        compiler_params=_params(1),
        name="convgate_sample",
    )(pg, pg, pg, prev, conv_w)


MOD_SH1, MOD_SC1, MOD_GT1, MOD_SH2, MOD_SC2, MOD_GT2 = range(6)


def _layer(x, rows, attend, convgate, ffn_gate, wts, da, dc, dff, q_scale=None):
    h = _prenorm(rows, x, wts["g_pre_mix"], MOD_SH1, MOD_SC1)
    if q_scale is None:
        q = _matmul(h, wts["w_in"], col0=0, ncols=da, name="proj_q")
        k = k_att = _matmul(h, wts["w_in"], col0=da, ncols=da, name="proj_k")
        v = v_att = _matmul(h, wts["w_in"], col0=2 * da, ncols=da, name="proj_v")
    else:
        q = _matmul(h, wts["w_in"], col0=0, ncols=da, out_dtypes=(BF16,), scale=q_scale, name="proj_q")
        k, k_att = _matmul(h, wts["w_in"], col0=da, ncols=da, out_dtypes=(F32, BF16), name="proj_k")
        v, v_att = _matmul(h, wts["w_in"], col0=2 * da, ncols=da, out_dtypes=(F32, BF16), name="proj_v")
    pg = _matmul(h, wts["w_in"], col0=3 * da, ncols=3 * dc, name="proj_conv")
    att = attend(q, k_att, v_att)
    short, conv_state = convgate(pg)
    merged = _merge_norm(rows, att, short, wts["g_attn_out"], wts["g_conv_out"])
    o = _matmul(merged, wts["w_o"], name="out_proj")
    x1, h2 = _resid_prenorm(rows, x, o, wts["g_post_mix"], MOD_GT1, wts["g_pre_ffn"], MOD_SH2, MOD_SC2)
    g, ffn_state = ffn_gate(h2)
    f = _matmul(g, wts["w_fc2"], name="fc2")
    y = _resid(rows, x1, f, wts["g_post_ffn"], MOD_GT2)
    return y, k, v, conv_state, ffn_state


def kernel(x_prompt, x_sample, c_prompt, c_sample, cache_k, cache_v, state_conv, state_ffn_conv, page_table, w_ada, b_ada, g_pre_mix, g_post_mix, w_in, sb_bias, conv_w, g_attn_out, g_conv_out, w_o, g_pre_ffn, g_post_ffn, w_fc1, ffn_conv_w, w_fc2):
    b, t, d = x_prompt.shape
    ns, nt, _ = x_sample.shape
    depth, n_phys, page, nh, dh = cache_k.shape
    assert depth == 1 and conv_w.shape[1] == CONV_WIDTH and ffn_conv_w.shape[1] == CONV_WIDTH
    da, dc, dff = nh * dh, conv_w.shape[-1], ffn_conv_w.shape[-1]

    wts = {
        "w_in": w_in.reshape(d, -1),
        "w_o": w_o.reshape(da + dc, d),
        "w_fc1": w_fc1.reshape(d, 2 * dff),
        "w_fc2": w_fc2.reshape(dff, d).astype(BF16),
        "g_pre_mix": g_pre_mix.reshape(1, d), "g_post_mix": g_post_mix.reshape(1, d),
        "g_attn_out": g_attn_out.reshape(1, da), "g_conv_out": g_conv_out.reshape(1, dc),
        "g_pre_ffn": g_pre_ffn.reshape(1, d), "g_post_ffn": g_post_ffn.reshape(1, d),
    }
    cw, fcw = conv_w.reshape(CONV_WIDTH, dc), ffn_conv_w.reshape(CONV_WIDTH, dff)
    bias = sb_bias.reshape(nh)

    n_c = ns + b
    pad = -n_c % 16
    c_all = jnp.concatenate([c_sample, c_prompt, jnp.zeros((pad, d), F32)], axis=0)
    mod = _ada(c_all, w_ada.reshape(d, 6 * d), b_ada.reshape(1, 6 * d))
    mod_s = mod[:ns].reshape(1, ns, 6 * d)
    mod_p = mod[ns:n_c].reshape(b, 1, 6 * d)

    tm_p = _tile(t, 256)
    rows_p = _Rows(b * t, tm_p, t // tm_p, mod_p)

    def attend_p(q, k, v):
        r3 = lambda a: a.reshape(b, t, da)
        return _prompt_attention(r3(q), r3(k), r3(v), bias, nh, dh).reshape(b * t, da)

    yp, kp, vp, cp, fp = _layer(
        x_prompt.reshape(b * t, d), rows_p, attend_p,
        lambda pg: _convgate_prompt(pg, cw, b, t, dc),
        lambda h2: _fc1_gate(h2, wts["w_fc1"], fcw, dff, tm=t),
        wts, da, dc, dff, q_scale=LOG2E * dh ** -0.5)

    rows_s = _Rows(nt * ns, ns, nt, mod_s)
    to_seq_major = lambda a: a.reshape(nt, ns, -1).transpose(1, 0, 2)
    to_time_major = lambda a: a.transpose(1, 0, 2).reshape(nt * ns, -1)
    kv_seq = {}

    def attend_s(q, k, v):
        kv_seq["k"], kv_seq["v"] = to_seq_major(k), to_seq_major(v)
        att = _sample_attention(to_seq_major(q), kv_seq["k"], kv_seq["v"],
                                cache_k.reshape(n_phys, page, nh, dh), cache_v.reshape(n_phys, page, nh, dh),
                                page_table, bias, nh, dh)
        return to_time_major(att)

    prev_conv = state_conv.reshape(ns, CONV_WIDTH - 1, dc).transpose(1, 0, 2)
    prev_ffn = state_ffn_conv.reshape(ns, CONV_WIDTH - 1, dff).transpose(1, 0, 2)
    ys, _, _, cs, fs = _layer(
        to_time_major(x_sample), rows_s, attend_s,
        lambda pg: _convgate_sample(pg, prev_conv, cw, nt, ns, dc),
        lambda h2: _fc1_gate(h2, wts["w_fc1"], fcw, dff, tm=nt * ns, prev=prev_ffn),
        wts, da, dc, dff)

    return (
        yp.reshape(b, t, d),
        to_seq_major(ys),
        kp.reshape(1, b, t, nh, dh),
        vp.reshape(1, b, t, nh, dh),
        cp.reshape(1, b, CONV_WIDTH - 1, dc),
        fp[:, 8 - (CONV_WIDTH - 1):, :].reshape(1, b, CONV_WIDTH - 1, dff),
        kv_seq["k"].reshape(1, ns, nt, nh, dh),
        kv_seq["v"].reshape(1, ns, nt, nh, dh),
        cs.transpose(1, 0, 2).reshape(1, ns, CONV_WIDTH - 1, dc),
        fs.transpose(1, 0, 2).reshape(1, ns, CONV_WIDTH - 1, dff),
    )
```

```python
import functools

import jax
import jax.numpy as jnp
from jax import lax
from jax.experimental import pallas as pl
from jax.experimental.pallas import tpu as pltpu

F32 = jnp.float32
BF16 = jnp.bfloat16
EPS = 1e-6
CONV_WIDTH = 3
V7X_VMEM_LIMIT_BYTES = 56 * 1024 * 1024
LANE = 128
NT_DIMS = (((1,), (1,)), ((), ()))


def _params(n_axes, vmem=V7X_VMEM_LIMIT_BYTES):
    return pltpu.CompilerParams(dimension_semantics=("arbitrary",) * n_axes, vmem_limit_bytes=vmem)


def _tile(dim, pref):
    if dim <= pref:
        return dim
    t = (pref // LANE) * LANE
    while dim % t:
        t -= LANE
    return t


def _rms(x):
    return x * lax.rsqrt(jnp.mean(x * x, axis=-1, keepdims=True) + EPS)


def _silu(x):
    return x * jax.nn.sigmoid(x)


def _ada_kernel(c_ref, w_ref, b_ref, o_ref, a_ref):
    @pl.when(pl.program_id(0) == 0)
    def _():
        a_ref[...] = _silu(c_ref[...]).astype(BF16)

    o_ref[...] = jnp.dot(a_ref[...], w_ref[...].astype(BF16), preferred_element_type=F32) + b_ref[...]


def _ada(c_all, w, b):
    rows, d = c_all.shape
    n = w.shape[1]
    tn = _tile(n, 512)
    return pl.pallas_call(
        _ada_kernel,
        grid=(n // tn,),
        in_specs=[
            pl.BlockSpec((rows, d), lambda j: (0, 0)),
            pl.BlockSpec((d, tn), lambda j: (0, j)),
            pl.BlockSpec((1, tn), lambda j: (0, j)),
        ],
        out_specs=pl.BlockSpec((rows, tn), lambda j: (0, j)),
        out_shape=jax.ShapeDtypeStruct((rows, n), F32),
        scratch_shapes=[pltpu.VMEM((rows, d), BF16)],
        compiler_params=_params(1),
        name="ada_mod",
    )(c_all, w, b)


class _Rows:
    def __init__(self, n_rows, tm, tiles_per_group, mod):
        self.n_rows, self.tm, self.tpg, self.mod = n_rows, tm, tiles_per_group, mod
        self.grid = (n_rows // tm,)

    def row(self, width, col=0):
        return pl.BlockSpec((self.tm, width), lambda i: (i, col))

    def vec(self, width):
        return pl.BlockSpec((1, width), lambda i: (0, 0))

    def modv(self, chunk, width):
        tpg = self.tpg
        return pl.BlockSpec((None, self.mod.shape[1], width), lambda i: (i // tpg, 0, chunk))


def _prenorm_kernel(x_ref, g_ref, sh_ref, sc_ref, h_ref):
    y = _rms(x_ref[...]) * g_ref[...]
    h_ref[...] = (y * (1.0 + sc_ref[...]) + sh_ref[...]).astype(h_ref.dtype)


def _prenorm(rows, x, g, sh_chunk, sc_chunk):
    d = x.shape[1]
    return pl.pallas_call(
        _prenorm_kernel,
        grid=rows.grid,
        in_specs=[rows.row(d), rows.vec(d), rows.modv(sh_chunk, d), rows.modv(sc_chunk, d)],
        out_specs=rows.row(d),
        out_shape=jax.ShapeDtypeStruct(x.shape, BF16),
        compiler_params=_params(1),
        name="prenorm",
    )(x, g, rows.mod, rows.mod)


def _merge_norm_kernel(att_ref, short_ref, ga_ref, gc_ref, o_ref):
    da = att_ref.shape[1]
    o_ref[:, :da] = (_rms(att_ref[...]) * ga_ref[...]).astype(o_ref.dtype)
    o_ref[:, da:] = (_rms(short_ref[...]) * gc_ref[...]).astype(o_ref.dtype)


def _merge_norm(rows, att, short, ga, gc):
    da, dc = att.shape[1], short.shape[1]
    return pl.pallas_call(
        _merge_norm_kernel,
        grid=rows.grid,
        in_specs=[rows.row(da), rows.row(dc), rows.vec(da), rows.vec(dc)],
        out_specs=rows.row(da + dc),
        out_shape=jax.ShapeDtypeStruct((att.shape[0], da + dc), BF16),
        compiler_params=_params(1),
        name="merge_norm",
    )(att, short, ga, gc)


def _resid_prenorm_kernel(x_ref, o_ref, gpost_ref, gt_ref, gpre_ref, sh_ref, sc_ref, x1_ref, h_ref):
    x1 = x_ref[...] + gt_ref[...] * (_rms(o_ref[...]) * gpost_ref[...])
    x1_ref[...] = x1
    y = _rms(x1) * gpre_ref[...]
    h_ref[...] = (y * (1.0 + sc_ref[...]) + sh_ref[...]).astype(h_ref.dtype)


def _resid_prenorm(rows, x, o, gpost, gt_chunk, gpre, sh_chunk, sc_chunk):
    d = x.shape[1]
    return pl.pallas_call(
        _resid_prenorm_kernel,
        grid=rows.grid,
        in_specs=[rows.row(d), rows.row(d), rows.vec(d), rows.modv(gt_chunk, d), rows.vec(d),
                  rows.modv(sh_chunk, d), rows.modv(sc_chunk, d)],
        out_specs=[rows.row(d), rows.row(d)],
        out_shape=[jax.ShapeDtypeStruct(x.shape, F32), jax.ShapeDtypeStruct(x.shape, BF16)],
        compiler_params=_params(1),
        name="resid_prenorm",
    )(x, o, gpost, rows.mod, gpre, rows.mod, rows.mod)


def _resid_kernel(x_ref, f_ref, gpost_ref, gt_ref, y_ref):
    y_ref[...] = x_ref[...] + gt_ref[...] * (_rms(f_ref[...]) * gpost_ref[...])


def _resid(rows, x, f, gpost, gt_chunk):
    d = x.shape[1]
    return pl.pallas_call(
        _resid_kernel,
        grid=rows.grid,
        in_specs=[rows.row(d), rows.row(d), rows.vec(d), rows.modv(gt_chunk, d)],
        out_specs=rows.row(d),
        out_shape=jax.ShapeDtypeStruct(x.shape, F32),
        compiler_params=_params(1),
        name="resid",
    )(x, f, gpost, rows.mod)


def _mm_kernel(a_ref, w_ref, *o_refs, nk, scale):
    acc = jnp.dot(a_ref[...], w_ref[...].astype(BF16), preferred_element_type=F32)
    if nk == 1:
        if scale is not None:
            acc = acc * scale
        for o_ref in o_refs:
            o_ref[...] = acc.astype(o_ref.dtype)
    else:
        o_ref, = o_refs
        k = pl.program_id(2)

        @pl.when(k == 0)
        def _():
            o_ref[...] = acc

        @pl.when(k != 0)
        def _():
            o_ref[...] += acc


def _matmul(a, w, *, col0=0, ncols=None, out_dtypes=(F32,), scale=None, tm_pref=1024, tn_pref=512, tk_pref=5632,
            name="matmul"):
    m, kdim = a.shape
    ncols = w.shape[1] if ncols is None else ncols
    tm, tn, tk = _tile(m, tm_pref), _tile(ncols, tn_pref), _tile(kdim, tk_pref)
    assert col0 % tn == 0
    cb, nk = col0 // tn, kdim // tk
    assert nk == 1 or (tuple(out_dtypes) == (F32,) and scale is None)
    outs = pl.pallas_call(
        functools.partial(_mm_kernel, nk=nk, scale=scale),
        grid=(m // tm, ncols // tn, nk),
        in_specs=[
            pl.BlockSpec((tm, tk), lambda i, j, k: (i, k)),
            pl.BlockSpec((tk, tn), lambda i, j, k: (k, j + cb)),
        ],
        out_specs=[pl.BlockSpec((tm, tn), lambda i, j, k: (i, j)) for _ in out_dtypes],
        out_shape=[jax.ShapeDtypeStruct((m, ncols), dt) for dt in out_dtypes],
        compiler_params=_params(3),
        name=name,
    )(a, w)
    return outs[0] if len(out_dtypes) == 1 else outs


def _proj_kernel(a_ref, w_ref, q_ref, k_ref, k16_ref, v_ref, v16_ref, g_ref, *, nq, q_scale):
    j = pl.program_id(1)

    def tile():
        return jnp.dot(a_ref[...], w_ref[...].astype(BF16), preferred_element_type=F32)

    @pl.when(j < nq)
    def _():
        q_ref[...] = (tile() * q_scale).astype(q_ref.dtype)

    @pl.when(jnp.logical_and(j >= nq, j < 2 * nq))
    def _():
        acc = tile()
        k_ref[...] = acc
        k16_ref[...] = acc.astype(k16_ref.dtype)

    @pl.when(jnp.logical_and(j >= 2 * nq, j < 3 * nq))
    def _():
        acc = tile()
        v_ref[...] = acc
        v16_ref[...] = acc.astype(v16_ref.dtype)

    @pl.when(j >= 3 * nq)
    def _():
        g_ref[...] = tile()


def _proj_all(h, w_in, da, dc, q_scale):
    m, d = h.shape
    tm, tn = _tile(m, 2048), _tile(da, 256)
    assert da % tn == 0 and (3 * dc) % tn == 0
    nq, ng = da // tn, 3 * dc // tn

    def seg(first, count):
        return pl.BlockSpec((tm, tn), lambda i, j: (i, jnp.clip(j - first, 0, count - 1)))

    return pl.pallas_call(
        functools.partial(_proj_kernel, nq=nq, q_scale=q_scale),
        grid=(m // tm, 3 * nq + ng),
        in_specs=[
            pl.BlockSpec((tm, d), lambda i, j: (i, 0), pipeline_mode=pl.Buffered(1)),
            pl.BlockSpec((d, tn), lambda i, j: (0, j)),
        ],
        out_specs=[seg(0, nq), seg(nq, nq), seg(nq, nq), seg(2 * nq, nq), seg(2 * nq, nq), seg(3 * nq, ng)],
        out_shape=[jax.ShapeDtypeStruct((m, da), BF16), jax.ShapeDtypeStruct((m, da), F32),
                   jax.ShapeDtypeStruct((m, da), BF16), jax.ShapeDtypeStruct((m, da), F32),
                   jax.ShapeDtypeStruct((m, da), BF16), jax.ShapeDtypeStruct((m, 3 * dc), F32)],
        compiler_params=_params(2),
        name="proj_all",
    )(h, w_in)


LOG2E = 1.4426950408889634


def _stick_chains(chains, u2):
    items = [(ci, bi) for ci, ch in enumerate(chains) for bi in range(len(ch[1]))]
    y, sp, drop, cum, a = {}, {}, {}, {}, {}
    for ci, bi in items:
        q, ks, _, bias, _, _, _ = chains[ci]
        y[ci, bi] = lax.dot_general(q, ks[bi], NT_DIMS, preferred_element_type=F32) + bias
    for it in items:
        sp[it] = jnp.maximum(y[it], 0.0) + jnp.log2(1.0 + jnp.exp2(-jnp.abs(y[it])))
        valid = chains[it[0]][6][it[1]]
        drop[it] = sp[it] if valid is None else jnp.where(valid, sp[it], 0.0)
    for it in items:
        hi = drop[it].astype(BF16)
        lo = (drop[it] - hi.astype(F32)).astype(BF16)
        cum[it] = jnp.dot(jnp.concatenate([hi, lo], axis=1), u2, preferred_element_type=F32)
    ds = [ch[4] for ch in chains]
    for it in items:
        ci = it[0]
        valid = chains[ci][6][it[1]]
        e = jnp.exp2((y[it] - sp[it]) - (cum[it] + ds[ci]))
        a[it] = (e if valid is None else jnp.where(valid, e, 0.0)).astype(BF16)
        ds[ci] = ds[ci] + jnp.sum(drop[it], axis=-1, keepdims=True)
    accs = [ch[5] for ch in chains]
    for ci, bi in items:
        accs[ci] = accs[ci] + jnp.dot(a[ci, bi], chains[ci][2][bi], preferred_element_type=F32)
    return list(zip(ds, accs))


def _strict_lower_twice(n):
    r = lax.broadcasted_iota(jnp.int32, (n, n), 0)
    c = lax.broadcasted_iota(jnp.int32, (n, n), 1)
    lower = (r > c).astype(BF16)
    return jnp.concatenate([lower, lower], axis=0)


PATTN_HEADS_PER_STEP = 8


def _pattn_kernel(bias_ref, q_ref, k_ref, v_ref, u_ref, o_ref, *, blk, dh, hps):
    hg, qi = pl.program_id(1), pl.program_id(2)
    u2 = u_ref[...]
    heads = [slice(i * dh, (i + 1) * dh) for i in range(hps)]
    qs = [q_ref[:, hd] for hd in heads]
    biases = [LOG2E * bias_ref[hg * hps + i] for i in range(hps)]

    def visit(kb, state, valid):
        start = pl.multiple_of(kb * blk, blk)
        chains = [(qs[i], [k_ref[pl.ds(start, blk), heads[i]]], [v_ref[pl.ds(start, blk), heads[i]]],
                   biases[i], state[i][0], state[i][1], [valid]) for i in range(hps)]
        return _stick_chains(chains, u2)

    row = lax.broadcasted_iota(jnp.int32, (blk, blk), 0)
    col = lax.broadcasted_iota(jnp.int32, (blk, blk), 1)
    state = [(jnp.zeros((blk, 1), F32), jnp.zeros((blk, dh), F32)) for _ in range(hps)]
    state = visit(qi, state, col < row)
    state = lax.fori_loop(0, qi, lambda it, st: visit(qi - 1 - it, st, None), state)
    for i in range(hps):
        o_ref[:, heads[i]] = state[i][1]


def _prompt_attention(q, k, v, bias, nh, dh):
    b, t, da = q.shape
    blk = _tile(t, 256)
    hps = min(PATTN_HEADS_PER_STEP, nh)
    assert nh % hps == 0
    u2 = _strict_lower_twice(blk)
    qspec = pl.BlockSpec((None, blk, hps * dh), lambda n, h, i: (n, i, h))
    kvspec = pl.BlockSpec((None, t, hps * dh), lambda n, h, i: (n, 0, h))
    return pl.pallas_call(
        functools.partial(_pattn_kernel, blk=blk, dh=dh, hps=hps),
        grid=(b, nh // hps, t // blk),
        in_specs=[
            pl.BlockSpec(memory_space=pltpu.SMEM),
            qspec, kvspec, kvspec,
            pl.BlockSpec((2 * blk, blk), lambda n, h, i: (0, 0)),
        ],
        out_specs=qspec,
        out_shape=jax.ShapeDtypeStruct((b, t, da), F32),
        compiler_params=_params(3),
        name="prompt_attn",
    )(bias, q, k, v, u2)


SATTN_PAGES_PER_STEP = 8
SATTN_HEADS_PER_BLOCK = 16


def _sattn_kernel(pt_ref, q_ref, kn_ref, vn_ref, *rest, nh, dh, nt, page, ppb, hpb, scale):
    nhb = nh // hpb
    n_ops = ppb * nhb
    k_refs, v_refs = rest[:n_ops], rest[n_ops:2 * n_ops]
    bias_ref, u_ref, o_ref, qbd_ref, d_ref, acc_ref, kpad_ref, vpad_ref, kbuf_ref, vbuf_ref = rest[2 * n_ops:]
    p = pl.program_id(1)
    da = nh * dh
    nrows = nt * nh
    log_nh, log_dh = nh.bit_length() - 1, dh.bit_length() - 1
    r = lax.broadcasted_iota(jnp.int32, (nrows, da), 0)
    lane = lax.broadcasted_iota(jnp.int32, (nrows, da), 1)
    own_head = (lane >> log_dh) == (r & (nh - 1))

    @pl.when(p == 0)
    def _():
        qrep = jnp.concatenate([jnp.broadcast_to(q_ref[t:t + 1, :], (nh, da)) for t in range(nt)], axis=0)
        qbd_ref[...] = jnp.where(own_head, qrep * (LOG2E * scale), 0.0).astype(BF16)
        kpad_ref[...] = jnp.zeros(kpad_ref.shape, F32)
        vpad_ref[...] = jnp.zeros(vpad_ref.shape, F32)
        kpad_ref[0:nt, :] = kn_ref[...]
        vpad_ref[0:nt, :] = vn_ref[...]
        rr = lax.broadcasted_iota(jnp.int32, (nrows, page), 0)
        cc = lax.broadcasted_iota(jnp.int32, (nrows, page), 1)
        valid = cc < (rr >> log_nh)
        u2_new = jnp.concatenate([u_ref[0:page, 0:page], u_ref[2 * page:3 * page, 0:page]], axis=0)
        (d, acc), = _stick_chains([(qbd_ref[...], [kpad_ref[...].astype(BF16)], [vpad_ref[...].astype(BF16)],
                                    bias_ref[:, 0:page], jnp.zeros((nrows, 1), F32), jnp.zeros((nrows, da), F32),
                                    [valid])], u2_new)
        d_ref[...] = d
        acc_ref[...] = acc

    def regroup(src_refs, dst_ref):
        for j in range(ppb):
            for hb in range(nhb):
                by_head = jnp.swapaxes(src_refs[j * nhb + hb][...].astype(BF16), 0, 1)
                for s in range(hpb):
                    head = hb * hpb + s
                    dst_ref[j * page:(j + 1) * page, head * dh:(head + 1) * dh] = by_head[s]

    regroup(k_refs, kbuf_ref)
    regroup(v_refs, vbuf_ref)

    chunk = 2 * page
    order = list(reversed(range(ppb * page // chunk)))
    (d, acc), = _stick_chains([(qbd_ref[...], [kbuf_ref[ci * chunk:(ci + 1) * chunk, :] for ci in order],
                                [vbuf_ref[ci * chunk:(ci + 1) * chunk, :] for ci in order],
                                bias_ref[...], d_ref[...], acc_ref[...], [None] * len(order))], u_ref[...])
    d_ref[...] = d
    acc_ref[...] = acc

    @pl.when(p == pl.num_programs(1) - 1)
    def _():
        own = jnp.where(own_head, acc, 0.0)
        o_ref[...] = jnp.sum(own.reshape(nt, nh, da), axis=1)


def _sample_attention(q, k_new, v_new, k_pages, v_pages, page_table, bias, nh, dh):
    ns, nt, da = q.shape
    page = k_pages.shape[1]
    npg = page_table.shape[1]
    ppb = min(SATTN_PAGES_PER_STEP, npg)
    hpb = min(SATTN_HEADS_PER_BLOCK, nh)
    assert npg % ppb == 0 and ppb % 2 == 0 and nh % hpb == 0
    assert nh & (nh - 1) == 0 and dh & (dh - 1) == 0 and nt <= page
    nrows = nt * nh
    u2 = _strict_lower_twice(2 * page)
    bias_rows = jnp.broadcast_to(jnp.tile(LOG2E * bias.astype(F32), nt)[:, None], (nrows, 2 * page))
    pt = page_table.reshape(-1).astype(jnp.int32)

    seq = pl.BlockSpec((None, nt, da), lambda n, p, pt: (n, 0, 0))

    def page_spec(j, hb):
        return pl.BlockSpec((None, page, hpb, dh),
                            lambda n, p, pt: (pt[n * npg + npg - ppb * (p + 1) + j], 0, hb, 0))

    page_specs = [page_spec(j, hb) for j in range(ppb) for hb in range(nh // hpb)]
    const = lambda shape: pl.BlockSpec(shape, lambda n, p, pt: (0, 0))
    grid_spec = pltpu.PrefetchScalarGridSpec(
        num_scalar_prefetch=1,
        grid=(ns, npg // ppb),
        in_specs=[seq, seq, seq] + page_specs + page_specs + [const((nrows, 2 * page)), const((4 * page, 2 * page))],
        out_specs=seq,
        scratch_shapes=[
            pltpu.VMEM((nrows, da), BF16),
            pltpu.VMEM((nrows, 1), F32),
            pltpu.VMEM((nrows, da), F32),
            pltpu.VMEM((page, da), F32),
            pltpu.VMEM((page, da), F32),
            pltpu.VMEM((ppb * page, da), BF16),
            pltpu.VMEM((ppb * page, da), BF16),
        ],
    )
    n_ops = len(page_specs)
    return pl.pallas_call(
        functools.partial(_sattn_kernel, nh=nh, dh=dh, nt=nt, page=page, ppb=ppb, hpb=hpb, scale=dh ** -0.5),
        grid_spec=grid_spec,
        out_shape=jax.ShapeDtypeStruct((ns, nt, da), F32),
        compiler_params=_params(2),
        name="sample_attn",
    )(pt, q, k_new, v_new, *([k_pages] * n_ops), *([v_pages] * n_ops), bias_rows, u2)


def _shift_rows(x, s):
    row = lax.broadcasted_iota(jnp.int32, x.shape, 0)
    return jnp.where(row >= s, pltpu.roll(x, s, 0), 0.0)


def _conv_rows(u, w_ref):
    return w_ref[0:1, :] * _shift_rows(u, 2) + w_ref[1:2, :] * _shift_rows(u, 1) + w_ref[2:3, :] * u


def _conv_time_major(us, prev_ref, w_ref):
    seq = [prev_ref[0], prev_ref[1]] + us
    ys = [w_ref[0:1, :] * seq[t] + w_ref[1:2, :] * seq[t + 1] + w_ref[2:3, :] * seq[t + 2] for t in range(len(us))]
    return ys, seq[-2:]


def _convgate_p_kernel(b_ref, c_ref, u_ref, w_ref, short_ref, st_ref):
    t = c_ref.shape[0]
    st_ref[...] = c_ref[t - 2:t, :] * u_ref[t - 2:t, :]
    short_ref[...] = b_ref[...] * _conv_rows(c_ref[...] * u_ref[...], w_ref)


def _convgate_prompt(pg, conv_w, n_seq, t, dc):
    tc = _tile(dc, 256)
    nj = dc // tc
    return pl.pallas_call(
        _convgate_p_kernel,
        grid=(n_seq, nj),
        in_specs=[
            pl.BlockSpec((t, tc), lambda n, j: (n, j)),
            pl.BlockSpec((t, tc), lambda n, j: (n, j + nj)),
            pl.BlockSpec((t, tc), lambda n, j: (n, j + 2 * nj)),
            pl.BlockSpec((CONV_WIDTH, tc), lambda n, j: (0, j)),
        ],
        out_specs=[pl.BlockSpec((t, tc), lambda n, j: (n, j)), pl.BlockSpec((None, 2, tc), lambda n, j: (n, 0, j))],
        out_shape=[jax.ShapeDtypeStruct((n_seq * t, dc), F32), jax.ShapeDtypeStruct((n_seq, 2, dc), F32)],
        compiler_params=_params(2),
        name="convgate_prompt",
    )(pg, pg, pg, conv_w)


FC1_ROW_CHUNKS = 4


def _conv_rows_after(u, tail, w_ref):
    row = lax.broadcasted_iota(jnp.int32, u.shape, 0)
    p1, p2 = tail[7:8, :], tail[6:7, :]
    s1 = jnp.where(row == 0, p1, pltpu.roll(u, 1, 0))
    s2 = jnp.where(row == 0, p2, jnp.where(row == 1, p1, pltpu.roll(u, 2, 0)))
    return w_ref[0:1, :] * s2 + w_ref[1:2, :] * s1 + w_ref[2:3, :] * u


def _fc1_gate_kernel(h_ref, wa_ref, wb_ref, cw_ref, *rest, prev_steps):
    wa = wa_ref[...].astype(BF16)
    wb = wb_ref[...].astype(BF16)
    if prev_steps == 0:
        g_ref, st_ref = rest
        t = h_ref.shape[0]
        n_chunks = FC1_ROW_CHUNKS if t % (8 * FC1_ROW_CHUNKS) == 0 else 1
        rc = t // n_chunks

        def dots(r):
            h = h_ref[r * rc:(r + 1) * rc, :]
            return jnp.dot(h, wa, preferred_element_type=F32), jnp.dot(h, wb, preferred_element_type=F32)

        def gate(r, ab, tail):
            a, b = ab
            g_ref[r * rc:(r + 1) * rc, :] = (_silu(_conv_rows_after(a, tail, cw_ref)) * b).astype(g_ref.dtype)

        tail = jnp.zeros((8, wa.shape[1]), F32)
        pending = dots(0)
        for r in range(1, n_chunks):
            nxt = dots(r)
            gate(r - 1, pending, tail)
            tail = pending[0][rc - 8:rc, :]
            pending = nxt
        gate(n_chunks - 1, pending, tail)
        st_ref[...] = pending[0][rc - 8:rc, :]
    else:
        prev_ref, g_ref, st_ref = rest
        h = h_ref[...]
        a = jnp.dot(h, wa, preferred_element_type=F32)
        b = jnp.dot(h, wb, preferred_element_type=F32)
        ns = prev_ref.shape[1]
        ys, last = _conv_time_major([a[t * ns:(t + 1) * ns, :] for t in range(prev_steps)], prev_ref, cw_ref)
        for t in range(prev_steps):
            g_ref[t * ns:(t + 1) * ns, :] = (_silu(ys[t]) * b[t * ns:(t + 1) * ns, :]).astype(g_ref.dtype)
        st_ref[0] = last[0]
        st_ref[1] = last[1]


def _fc1_gate(h, w_fc1, conv_w, dff, *, tm, prev=None):
    m, d = h.shape
    tn = _tile(dff, 256)
    nj = dff // tn
    in_specs = [
        pl.BlockSpec((tm, d), lambda i, j: (i, 0), pipeline_mode=pl.Buffered(1)),
        pl.BlockSpec((d, tn), lambda i, j: (0, j)),
        pl.BlockSpec((d, tn), lambda i, j: (0, j + nj)),
        pl.BlockSpec((CONV_WIDTH, tn), lambda i, j: (0, j)),
    ]
    args = [h, w_fc1, w_fc1, conv_w]
    if prev is None:
        prev_steps = 0
        st_spec = pl.BlockSpec((None, 8, tn), lambda i, j: (i, 0, j))
        st_shape = jax.ShapeDtypeStruct((m // tm, 8, dff), F32)
    else:
        ns = prev.shape[1]
        prev_steps = m // ns
        assert tm == m
        in_specs.append(pl.BlockSpec((2, ns, tn), lambda i, j: (0, 0, j)))
        args.append(prev)
        st_spec = pl.BlockSpec((2, ns, tn), lambda i, j: (0, 0, j))
        st_shape = jax.ShapeDtypeStruct((2, ns, dff), F32)
    return pl.pallas_call(
        functools.partial(_fc1_gate_kernel, prev_steps=prev_steps),
        grid=(m // tm, nj),
        in_specs=in_specs,
        out_specs=[pl.BlockSpec((tm, tn), lambda i, j: (i, j)), st_spec],
        out_shape=[jax.ShapeDtypeStruct((m, dff), BF16), st_shape],
        compiler_params=_params(2),
        name="fc1_gate",
    )(*args)


def _convgate_s_kernel(b_ref, c_ref, u_ref, prev_ref, w_ref, short_ref, st_ref, *, nt):
    ns = prev_ref.shape[1]
    cu = [c_ref[t * ns:(t + 1) * ns, :] * u_ref[t * ns:(t + 1) * ns, :] for t in range(nt)]
    ys, last = _conv_time_major(cu, prev_ref, w_ref)
    for t in range(nt):
        short_ref[t * ns:(t + 1) * ns, :] = b_ref[t * ns:(t + 1) * ns, :] * ys[t]
    st_ref[0] = last[0]
    st_ref[1] = last[1]


def _convgate_sample(pg, prev, conv_w, nt, ns, dc):
    tc = _tile(dc, 512)
    nj = dc // tc
    rows = nt * ns
    return pl.pallas_call(
        functools.partial(_convgate_s_kernel, nt=nt),
        grid=(nj,),
        in_specs=[
            pl.BlockSpec((rows, tc), lambda j: (0, j)),
            pl.BlockSpec((rows, tc), lambda j: (0, j + nj)),
            pl.BlockSpec((rows, tc), lambda j: (0, j + 2 * nj)),
            pl.BlockSpec((2, ns, tc), lambda j: (0, 0, j)),
            pl.BlockSpec((CONV_WIDTH, tc), lambda j: (0, j)),
        ],
        out_specs=[pl.BlockSpec((rows, tc), lambda j: (0, j)), pl.BlockSpec((2, ns, tc), lambda j: (0, 0, j))],
        out_shape=[jax.ShapeDtypeStruct((rows, dc), F32), jax.ShapeDtypeStruct((2, ns, dc), F32)],
        compiler_params=_params(1),
        name="convgate_sample",
    )(pg, pg, pg, prev, conv_w)


MOD_SH1, MOD_SC1, MOD_GT1, MOD_SH2, MOD_SC2, MOD_GT2 = range(6)


def _layer(x, rows, attend, convgate, ffn_gate, wts, da, dc, dff, q_scale=None):
    h = _prenorm(rows, x, wts["g_pre_mix"], MOD_SH1, MOD_SC1)
    if q_scale is None:
        q = _matmul(h, wts["w_in"], col0=0, ncols=da, name="proj_q")
        k = k_att = _matmul(h, wts["w_in"], col0=da, ncols=da, name="proj_k")
        v = v_att = _matmul(h, wts["w_in"], col0=2 * da, ncols=da, name="proj_v")
        pg = _matmul(h, wts["w_in"], col0=3 * da, ncols=3 * dc, name="proj_conv")
    else:
        q, k, k_att, v, v_att, pg = _proj_all(h, wts["w_in"], da, dc, q_scale)
    att = attend(q, k_att, v_att)
    short, conv_state = convgate(pg)
    merged = _merge_norm(rows, att, short, wts["g_attn_out"], wts["g_conv_out"])
    o = _matmul(merged, wts["w_o"], name="out_proj")
    x1, h2 = _resid_prenorm(rows, x, o, wts["g_post_mix"], MOD_GT1, wts["g_pre_ffn"], MOD_SH2, MOD_SC2)
    g, ffn_state = ffn_gate(h2)
    f = _matmul(g, wts["w_fc2"], name="fc2")
    y = _resid(rows, x1, f, wts["g_post_ffn"], MOD_GT2)
    return y, k, v, conv_state, ffn_state


def kernel(x_prompt, x_sample, c_prompt, c_sample, cache_k, cache_v, state_conv, state_ffn_conv, page_table, w_ada, b_ada, g_pre_mix, g_post_mix, w_in, sb_bias, conv_w, g_attn_out, g_conv_out, w_o, g_pre_ffn, g_post_ffn, w_fc1, ffn_conv_w, w_fc2):
    b, t, d = x_prompt.shape
    ns, nt, _ = x_sample.shape
    depth, n_phys, page, nh, dh = cache_k.shape
    assert depth == 1 and conv_w.shape[1] == CONV_WIDTH and ffn_conv_w.shape[1] == CONV_WIDTH
    da, dc, dff = nh * dh, conv_w.shape[-1], ffn_conv_w.shape[-1]

    wts = {
        "w_in": w_in.reshape(d, -1),
        "w_o": w_o.reshape(da + dc, d),
        "w_fc1": w_fc1.reshape(d, 2 * dff),
        "w_fc2": w_fc2.reshape(dff, d).astype(BF16),
        "g_pre_mix": g_pre_mix.reshape(1, d), "g_post_mix": g_post_mix.reshape(1, d),
        "g_attn_out": g_attn_out.reshape(1, da), "g_conv_out": g_conv_out.reshape(1, dc),
        "g_pre_ffn": g_pre_ffn.reshape(1, d), "g_post_ffn": g_post_ffn.reshape(1, d),
    }
    cw, fcw = conv_w.reshape(CONV_WIDTH, dc), ffn_conv_w.reshape(CONV_WIDTH, dff)
    bias = sb_bias.reshape(nh)

    n_c = ns + b
    pad = -n_c % 16
    c_all = jnp.concatenate([c_sample, c_prompt, jnp.zeros((pad, d), F32)], axis=0)
    mod = _ada(c_all, w_ada.reshape(d, 6 * d), b_ada.reshape(1, 6 * d))
    mod_s = mod[:ns].reshape(1, ns, 6 * d)
    mod_p = mod[ns:n_c].reshape(b, 1, 6 * d)

    tm_p = _tile(t, 256)
    rows_p = _Rows(b * t, tm_p, t // tm_p, mod_p)

    def attend_p(q, k, v):
        r3 = lambda a: a.reshape(b, t, da)
        return _prompt_attention(r3(q), r3(k), r3(v), bias, nh, dh).reshape(b * t, da)

    yp, kp, vp, cp, fp = _layer(
        x_prompt.reshape(b * t, d), rows_p, attend_p,
        lambda pg: _convgate_prompt(pg, cw, b, t, dc),
        lambda h2: _fc1_gate(h2, wts["w_fc1"], fcw, dff, tm=t),
        wts, da, dc, dff, q_scale=LOG2E * dh ** -0.5)

    rows_s = _Rows(nt * ns, ns, nt, mod_s)
    to_seq_major = lambda a: a.reshape(nt, ns, -1).transpose(1, 0, 2)
    to_time_major = lambda a: a.transpose(1, 0, 2).reshape(nt * ns, -1)
    kv_seq = {}

    def attend_s(q, k, v):
        kv_seq["k"], kv_seq["v"] = to_seq_major(k), to_seq_major(v)
        att = _sample_attention(to_seq_major(q), kv_seq["k"], kv_seq["v"],
                                cache_k.reshape(n_phys, page, nh, dh), cache_v.reshape(n_phys, page, nh, dh),
                                page_table, bias, nh, dh)
        return to_time_major(att)

    prev_conv = state_conv.reshape(ns, CONV_WIDTH - 1, dc).transpose(1, 0, 2)
    prev_ffn = state_ffn_conv.reshape(ns, CONV_WIDTH - 1, dff).transpose(1, 0, 2)
    ys, _, _, cs, fs = _layer(
        to_time_major(x_sample), rows_s, attend_s,
        lambda pg: _convgate_sample(pg, prev_conv, cw, nt, ns, dc),
        lambda h2: _fc1_gate(h2, wts["w_fc1"], fcw, dff, tm=nt * ns, prev=prev_ffn),
        wts, da, dc, dff)

    return (
        yp.reshape(b, t, d),
        to_seq_major(ys),
        kp.reshape(1, b, t, nh, dh),
        vp.reshape(1, b, t, nh, dh),
        cp.reshape(1, b, CONV_WIDTH - 1, dc),
        fp[:, 8 - (CONV_WIDTH - 1):, :].reshape(1, b, CONV_WIDTH - 1, dff),
        kv_seq["k"].reshape(1, ns, nt, nh, dh),
        kv_seq["v"].reshape(1, ns, nt, nh, dh),
        cs.transpose(1, 0, 2).reshape(1, ns, CONV_WIDTH - 1, dc),
        fs.transpose(1, 0, 2).reshape(1, ns, CONV_WIDTH - 1, dff),
    )
```

```python
import functools

import jax
import jax.numpy as jnp
from jax import lax
from jax.experimental import pallas as pl
from jax.experimental.pallas import tpu as pltpu

F32 = jnp.float32
BF16 = jnp.bfloat16
EPS = 1e-6
CONV_WIDTH = 3
V7X_VMEM_LIMIT_BYTES = 56 * 1024 * 1024
LANE = 128
NT_DIMS = (((1,), (1,)), ((), ()))


def _params(n_axes, vmem=V7X_VMEM_LIMIT_BYTES):
    return pltpu.CompilerParams(dimension_semantics=("arbitrary",) * n_axes, vmem_limit_bytes=vmem)


def _tile(dim, pref):
    if dim <= pref:
        return dim
    t = (pref // LANE) * LANE
    while dim % t:
        t -= LANE
    return t


def _rms(x):
    return x * lax.rsqrt(jnp.mean(x * x, axis=-1, keepdims=True) + EPS)


def _silu(x):
    return x * jax.nn.sigmoid(x)


def _ada_kernel(c_ref, w_ref, b_ref, o_ref, a_ref):
    @pl.when(pl.program_id(0) == 0)
    def _():
        a_ref[...] = _silu(c_ref[...]).astype(BF16)

    o_ref[...] = jnp.dot(a_ref[...], w_ref[...].astype(BF16), preferred_element_type=F32) + b_ref[...]


def _ada(c_all, w, b):
    rows, d = c_all.shape
    n = w.shape[1]
    tn = _tile(n, 1024)
    return pl.pallas_call(
        _ada_kernel,
        grid=(n // tn,),
        in_specs=[
            pl.BlockSpec((rows, d), lambda j: (0, 0)),
            pl.BlockSpec((d, tn), lambda j: (0, j)),
            pl.BlockSpec((1, tn), lambda j: (0, j)),
        ],
        out_specs=pl.BlockSpec((rows, tn), lambda j: (0, j)),
        out_shape=jax.ShapeDtypeStruct((rows, n), F32),
        scratch_shapes=[pltpu.VMEM((rows, d), BF16)],
        compiler_params=_params(1),
        name="ada_mod",
    )(c_all, w, b)


class _Rows:
    def __init__(self, n_rows, tm, tiles_per_group, mod):
        self.n_rows, self.tm, self.tpg, self.mod = n_rows, tm, tiles_per_group, mod
        self.grid = (n_rows // tm,)

    def row(self, width, col=0):
        return pl.BlockSpec((self.tm, width), lambda i: (i, col))

    def vec(self, width):
        return pl.BlockSpec((1, width), lambda i: (0, 0))

    def modv(self, chunk, width):
        tpg = self.tpg
        return pl.BlockSpec((None, self.mod.shape[1], width), lambda i: (i // tpg, 0, chunk))


def _prenorm_kernel(x_ref, g_ref, sh_ref, sc_ref, h_ref):
    y = _rms(x_ref[...]) * g_ref[...]
    h_ref[...] = (y * (1.0 + sc_ref[...]) + sh_ref[...]).astype(h_ref.dtype)


def _prenorm(rows, x, g, sh_chunk, sc_chunk):
    d = x.shape[1]
    return pl.pallas_call(
        _prenorm_kernel,
        grid=rows.grid,
        in_specs=[rows.row(d), rows.vec(d), rows.modv(sh_chunk, d), rows.modv(sc_chunk, d)],
        out_specs=rows.row(d),
        out_shape=jax.ShapeDtypeStruct(x.shape, BF16),
        compiler_params=_params(1),
        name="prenorm",
    )(x, g, rows.mod, rows.mod)


def _merge_norm_kernel(att_ref, short_ref, ga_ref, gc_ref, o_ref):
    da = att_ref.shape[1]
    o_ref[:, :da] = (_rms(att_ref[...]) * ga_ref[...]).astype(o_ref.dtype)
    o_ref[:, da:] = (_rms(short_ref[...]) * gc_ref[...]).astype(o_ref.dtype)


def _merge_norm(rows, att, short, ga, gc):
    da, dc = att.shape[1], short.shape[1]
    return pl.pallas_call(
        _merge_norm_kernel,
        grid=rows.grid,
        in_specs=[rows.row(da), rows.row(dc), rows.vec(da), rows.vec(dc)],
        out_specs=rows.row(da + dc),
        out_shape=jax.ShapeDtypeStruct((att.shape[0], da + dc), BF16),
        compiler_params=_params(1),
        name="merge_norm",
    )(att, short, ga, gc)


def _resid_prenorm_kernel(x_ref, o_ref, gpost_ref, gt_ref, gpre_ref, sh_ref, sc_ref, x1_ref, h_ref):
    x1 = x_ref[...] + gt_ref[...] * (_rms(o_ref[...]) * gpost_ref[...])
    x1_ref[...] = x1
    y = _rms(x1) * gpre_ref[...]
    h_ref[...] = (y * (1.0 + sc_ref[...]) + sh_ref[...]).astype(h_ref.dtype)


def _resid_prenorm(rows, x, o, gpost, gt_chunk, gpre, sh_chunk, sc_chunk):
    d = x.shape[1]
    return pl.pallas_call(
        _resid_prenorm_kernel,
        grid=rows.grid,
        in_specs=[rows.row(d), rows.row(d), rows.vec(d), rows.modv(gt_chunk, d), rows.vec(d),
                  rows.modv(sh_chunk, d), rows.modv(sc_chunk, d)],
        out_specs=[rows.row(d), rows.row(d)],
        out_shape=[jax.ShapeDtypeStruct(x.shape, F32), jax.ShapeDtypeStruct(x.shape, BF16)],
        compiler_params=_params(1),
        name="resid_prenorm",
    )(x, o, gpost, rows.mod, gpre, rows.mod, rows.mod)


def _resid_kernel(x_ref, f_ref, gpost_ref, gt_ref, y_ref):
    y_ref[...] = x_ref[...] + gt_ref[...] * (_rms(f_ref[...]) * gpost_ref[...])


def _resid(rows, x, f, gpost, gt_chunk):
    d = x.shape[1]
    return pl.pallas_call(
        _resid_kernel,
        grid=rows.grid,
        in_specs=[rows.row(d), rows.row(d), rows.vec(d), rows.modv(gt_chunk, d)],
        out_specs=rows.row(d),
        out_shape=jax.ShapeDtypeStruct(x.shape, F32),
        compiler_params=_params(1),
        name="resid",
    )(x, f, gpost, rows.mod)


def _mm_kernel(a_ref, w_ref, *o_refs, nk, scale):
    acc = jnp.dot(a_ref[...], w_ref[...].astype(BF16), preferred_element_type=F32)
    if nk == 1:
        if scale is not None:
            acc = acc * scale
        for o_ref in o_refs:
            o_ref[...] = acc.astype(o_ref.dtype)
    else:
        o_ref, = o_refs
        k = pl.program_id(2)

        @pl.when(k == 0)
        def _():
            o_ref[...] = acc

        @pl.when(k != 0)
        def _():
            o_ref[...] += acc


def _matmul(a, w, *, col0=0, ncols=None, out_dtypes=(F32,), scale=None, tm_pref=1024, tn_pref=512, tk_pref=5632,
            name="matmul"):
    m, kdim = a.shape
    ncols = w.shape[1] if ncols is None else ncols
    tm, tn, tk = _tile(m, tm_pref), _tile(ncols, tn_pref), _tile(kdim, tk_pref)
    assert col0 % tn == 0
    cb, nk = col0 // tn, kdim // tk
    assert nk == 1 or (tuple(out_dtypes) == (F32,) and scale is None)
    outs = pl.pallas_call(
        functools.partial(_mm_kernel, nk=nk, scale=scale),
        grid=(m // tm, ncols // tn, nk),
        in_specs=[
            pl.BlockSpec((tm, tk), lambda i, j, k: (i, k)),
            pl.BlockSpec((tk, tn), lambda i, j, k: (k, j + cb)),
        ],
        out_specs=[pl.BlockSpec((tm, tn), lambda i, j, k: (i, j)) for _ in out_dtypes],
        out_shape=[jax.ShapeDtypeStruct((m, ncols), dt) for dt in out_dtypes],
        compiler_params=_params(3),
        name=name,
    )(a, w)
    return outs[0] if len(out_dtypes) == 1 else outs


def _proj_kernel(a_ref, w_ref, q_ref, k_ref, k16_ref, v_ref, v16_ref, g_ref, *, nq, q_scale):
    j = pl.program_id(1)

    def tile():
        return jnp.dot(a_ref[...], w_ref[...].astype(BF16), preferred_element_type=F32)

    @pl.when(j < nq)
    def _():
        acc = tile()
        q_ref[...] = (acc if q_scale is None else acc * q_scale).astype(q_ref.dtype)

    @pl.when(jnp.logical_and(j >= nq, j < 2 * nq))
    def _():
        acc = tile()
        k_ref[...] = acc
        k16_ref[...] = acc.astype(k16_ref.dtype)

    @pl.when(jnp.logical_and(j >= 2 * nq, j < 3 * nq))
    def _():
        acc = tile()
        v_ref[...] = acc
        v16_ref[...] = acc.astype(v16_ref.dtype)

    @pl.when(j >= 3 * nq)
    def _():
        g_ref[...] = tile()


def _proj_all(h, w_in, da, dc, q_scale):
    m, d = h.shape
    tm, tn = _tile(m, 2048), _tile(da, 256)
    assert da % tn == 0 and (3 * dc) % tn == 0
    nq, ng = da // tn, 3 * dc // tn

    def seg(first, count):
        return pl.BlockSpec((tm, tn), lambda i, j: (i, jnp.clip(j - first, 0, count - 1)))

    return pl.pallas_call(
        functools.partial(_proj_kernel, nq=nq, q_scale=q_scale),
        grid=(m // tm, 3 * nq + ng),
        in_specs=[
            pl.BlockSpec((tm, d), lambda i, j: (i, 0), pipeline_mode=pl.Buffered(1)),
            pl.BlockSpec((d, tn), lambda i, j: (0, j)),
        ],
        out_specs=[seg(0, nq), seg(nq, nq), seg(nq, nq), seg(2 * nq, nq), seg(2 * nq, nq), seg(3 * nq, ng)],
        out_shape=[jax.ShapeDtypeStruct((m, da), F32 if q_scale is None else BF16), jax.ShapeDtypeStruct((m, da), F32),
                   jax.ShapeDtypeStruct((m, da), BF16), jax.ShapeDtypeStruct((m, da), F32),
                   jax.ShapeDtypeStruct((m, da), BF16), jax.ShapeDtypeStruct((m, 3 * dc), F32)],
        compiler_params=_params(2),
        name="proj_all",
    )(h, w_in)


LOG2E = 1.4426950408889634


def _stick_chains(chains, u2):
    items = [(ci, bi) for ci, ch in enumerate(chains) for bi in range(len(ch[1]))]
    y, sp, drop, cum, a = {}, {}, {}, {}, {}
    for ci, bi in items:
        q, ks, _, bias, _, _, _ = chains[ci]
        y[ci, bi] = lax.dot_general(q, ks[bi], NT_DIMS, preferred_element_type=F32) + bias
    for it in items:
        sp[it] = jnp.maximum(y[it], 0.0) + jnp.log2(1.0 + jnp.exp2(-jnp.abs(y[it])))
        valid = chains[it[0]][6][it[1]]
        drop[it] = sp[it] if valid is None else jnp.where(valid, sp[it], 0.0)
    for it in items:
        hi = drop[it].astype(BF16)
        lo = (drop[it] - hi.astype(F32)).astype(BF16)
        cum[it] = jnp.dot(jnp.concatenate([hi, lo], axis=1), u2, preferred_element_type=F32)
    ds = [ch[4] for ch in chains]
    for it in items:
        ci = it[0]
        valid = chains[ci][6][it[1]]
        e = jnp.exp2((y[it] - sp[it]) - (cum[it] + ds[ci]))
        a[it] = (e if valid is None else jnp.where(valid, e, 0.0)).astype(BF16)
        ds[ci] = ds[ci] + jnp.sum(drop[it], axis=-1, keepdims=True)
    accs = [ch[5] for ch in chains]
    for ci, bi in items:
        accs[ci] = accs[ci] + jnp.dot(a[ci, bi], chains[ci][2][bi], preferred_element_type=F32)
    return list(zip(ds, accs))


def _strict_lower_twice(n):
    r = lax.broadcasted_iota(jnp.int32, (n, n), 0)
    c = lax.broadcasted_iota(jnp.int32, (n, n), 1)
    lower = (r > c).astype(BF16)
    return jnp.concatenate([lower, lower], axis=0)


PATTN_HEADS_PER_STEP = 8


def _pattn_kernel(bias_ref, q_ref, k_ref, v_ref, u_ref, o_ref, *, blk, dh, hps):
    hg, qi = pl.program_id(1), pl.program_id(2)
    u2 = u_ref[...]
    heads = [slice(i * dh, (i + 1) * dh) for i in range(hps)]
    qs = [q_ref[:, hd] for hd in heads]
    biases = [LOG2E * bias_ref[hg * hps + i] for i in range(hps)]

    def visit(kbs, state, valids):
        starts = [pl.multiple_of(kb * blk, blk) for kb in kbs]
        chains = [(qs[i], [k_ref[pl.ds(st, blk), heads[i]] for st in starts],
                   [v_ref[pl.ds(st, blk), heads[i]] for st in starts],
                   biases[i], state[i][0], state[i][1], valids) for i in range(hps)]
        return _stick_chains(chains, u2)

    row = lax.broadcasted_iota(jnp.int32, (blk, blk), 0)
    col = lax.broadcasted_iota(jnp.int32, (blk, blk), 1)
    diag = col < row
    state = [(jnp.zeros((blk, 1), F32), jnp.zeros((blk, dh), F32)) for _ in range(hps)]
    odd = qi & 1
    state = lax.cond(odd == 1, lambda: visit([qi, qi - 1], state, [diag, None]), lambda: visit([qi], state, [diag]))
    nxt = qi - 1 - odd
    state = lax.fori_loop(0, lax.shift_right_logical(qi, 1),
                          lambda it, st: visit([nxt - 2 * it, nxt - 2 * it - 1], st, [None, None]), state)
    for i in range(hps):
        o_ref[:, heads[i]] = state[i][1]


def _prompt_attention(q, k, v, bias, nh, dh):
    b, t, da = q.shape
    blk = _tile(t, 256)
    hps = min(PATTN_HEADS_PER_STEP, nh)
    assert nh % hps == 0
    u2 = _strict_lower_twice(blk)
    qspec = pl.BlockSpec((None, blk, hps * dh), lambda n, h, i: (n, i, h))
    kvspec = pl.BlockSpec((None, t, hps * dh), lambda n, h, i: (n, 0, h))
    return pl.pallas_call(
        functools.partial(_pattn_kernel, blk=blk, dh=dh, hps=hps),
        grid=(b, nh // hps, t // blk),
        in_specs=[
            pl.BlockSpec(memory_space=pltpu.SMEM),
            qspec, kvspec, kvspec,
            pl.BlockSpec((2 * blk, blk), lambda n, h, i: (0, 0)),
        ],
        out_specs=qspec,
        out_shape=jax.ShapeDtypeStruct((b, t, da), F32),
        compiler_params=_params(3),
        name="prompt_attn",
    )(bias, q, k, v, u2)


SATTN_PAGES_PER_STEP = 8
SATTN_HEADS_PER_BLOCK = 16


def _sattn_kernel(pt_ref, q_ref, kn_ref, vn_ref, *rest, nh, dh, nt, page, ppb, hpb, scale):
    nhb = nh // hpb
    n_ops = ppb * nhb
    k_refs, v_refs = rest[:n_ops], rest[n_ops:2 * n_ops]
    bias_ref, u_ref, o_ref, qbd_ref, d_ref, acc_ref, kpad_ref, vpad_ref, kbuf_ref, vbuf_ref = rest[2 * n_ops:]
    p = pl.program_id(1)
    da = nh * dh
    nrows = nt * nh
    log_nh, log_dh = nh.bit_length() - 1, dh.bit_length() - 1
    r = lax.broadcasted_iota(jnp.int32, (nrows, da), 0)
    lane = lax.broadcasted_iota(jnp.int32, (nrows, da), 1)
    own_head = (lane >> log_dh) == (r & (nh - 1))

    @pl.when(p == 0)
    def _():
        qrep = jnp.concatenate([jnp.broadcast_to(q_ref[t:t + 1, :], (nh, da)) for t in range(nt)], axis=0)
        qbd_ref[...] = jnp.where(own_head, qrep * (LOG2E * scale), 0.0).astype(BF16)
        kpad_ref[...] = jnp.zeros(kpad_ref.shape, F32)
        vpad_ref[...] = jnp.zeros(vpad_ref.shape, F32)
        kpad_ref[0:nt, :] = kn_ref[...]
        vpad_ref[0:nt, :] = vn_ref[...]
        rr = lax.broadcasted_iota(jnp.int32, (nrows, page), 0)
        cc = lax.broadcasted_iota(jnp.int32, (nrows, page), 1)
        valid = cc < (rr >> log_nh)
        u2_new = jnp.concatenate([u_ref[0:page, 0:page], u_ref[2 * page:3 * page, 0:page]], axis=0)
        (d, acc), = _stick_chains([(qbd_ref[...], [kpad_ref[...].astype(BF16)], [vpad_ref[...].astype(BF16)],
                                    bias_ref[:, 0:page], jnp.zeros((nrows, 1), F32), jnp.zeros((nrows, da), F32),
                                    [valid])], u2_new)
        d_ref[...] = d
        acc_ref[...] = acc

    def regroup(src_refs, dst_ref):
        for j in range(ppb):
            for hb in range(nhb):
                by_head = jnp.swapaxes(src_refs[j * nhb + hb][...].astype(BF16), 0, 1)
                for s in range(hpb):
                    head = hb * hpb + s
                    dst_ref[j * page:(j + 1) * page, head * dh:(head + 1) * dh] = by_head[s]

    regroup(k_refs, kbuf_ref)
    regroup(v_refs, vbuf_ref)

    chunk = 2 * page
    order = list(reversed(range(ppb * page // chunk)))
    (d, acc), = _stick_chains([(qbd_ref[...], [kbuf_ref[ci * chunk:(ci + 1) * chunk, :] for ci in order],
                                [vbuf_ref[ci * chunk:(ci + 1) * chunk, :] for ci in order],
                                bias_ref[...], d_ref[...], acc_ref[...], [None] * len(order))], u_ref[...])
    d_ref[...] = d
    acc_ref[...] = acc

    @pl.when(p == pl.num_programs(1) - 1)
    def _():
        own = jnp.where(own_head, acc, 0.0)
        o_ref[...] = jnp.sum(own.reshape(nt, nh, da), axis=1)


def _sample_attention(q, k_new, v_new, k_pages, v_pages, page_table, bias, nh, dh):
    ns, nt, da = q.shape
    page = k_pages.shape[1]
    npg = page_table.shape[1]
    ppb = min(SATTN_PAGES_PER_STEP, npg)
    hpb = min(SATTN_HEADS_PER_BLOCK, nh)
    assert npg % ppb == 0 and ppb % 2 == 0 and nh % hpb == 0
    assert nh & (nh - 1) == 0 and dh & (dh - 1) == 0 and nt <= page
    nrows = nt * nh
    u2 = _strict_lower_twice(2 * page)
    bias_rows = jnp.broadcast_to(jnp.tile(LOG2E * bias.astype(F32), nt)[:, None], (nrows, 2 * page))
    pt = page_table.reshape(-1).astype(jnp.int32)

    seq = pl.BlockSpec((None, nt, da), lambda n, p, pt: (n, 0, 0))

    def page_spec(j, hb):
        return pl.BlockSpec((None, page, hpb, dh),
                            lambda n, p, pt: (pt[n * npg + npg - ppb * (p + 1) + j], 0, hb, 0))

    page_specs = [page_spec(j, hb) for j in range(ppb) for hb in range(nh // hpb)]
    const = lambda shape: pl.BlockSpec(shape, lambda n, p, pt: (0, 0))
    grid_spec = pltpu.PrefetchScalarGridSpec(
        num_scalar_prefetch=1,
        grid=(ns, npg // ppb),
        in_specs=[seq, seq, seq] + page_specs + page_specs + [const((nrows, 2 * page)), const((4 * page, 2 * page))],
        out_specs=seq,
        scratch_shapes=[
            pltpu.VMEM((nrows, da), BF16),
            pltpu.VMEM((nrows, 1), F32),
            pltpu.VMEM((nrows, da), F32),
            pltpu.VMEM((page, da), F32),
            pltpu.VMEM((page, da), F32),
            pltpu.VMEM((ppb * page, da), BF16),
            pltpu.VMEM((ppb * page, da), BF16),
        ],
    )
    n_ops = len(page_specs)
    return pl.pallas_call(
        functools.partial(_sattn_kernel, nh=nh, dh=dh, nt=nt, page=page, ppb=ppb, hpb=hpb, scale=dh ** -0.5),
        grid_spec=grid_spec,
        out_shape=jax.ShapeDtypeStruct((ns, nt, da), F32),
        compiler_params=_params(2),
        name="sample_attn",
    )(pt, q, k_new, v_new, *([k_pages] * n_ops), *([v_pages] * n_ops), bias_rows, u2)


def _shift_rows(x, s):
    row = lax.broadcasted_iota(jnp.int32, x.shape, 0)
    return jnp.where(row >= s, pltpu.roll(x, s, 0), 0.0)


def _conv_rows(u, w_ref):
    return w_ref[0:1, :] * _shift_rows(u, 2) + w_ref[1:2, :] * _shift_rows(u, 1) + w_ref[2:3, :] * u


def _conv_time_major(us, prev_ref, w_ref):
    seq = [prev_ref[0], prev_ref[1]] + us
    ys = [w_ref[0:1, :] * seq[t] + w_ref[1:2, :] * seq[t + 1] + w_ref[2:3, :] * seq[t + 2] for t in range(len(us))]
    return ys, seq[-2:]


def _convgate_p_kernel(b_ref, c_ref, u_ref, w_ref, short_ref, st_ref):
    t = c_ref.shape[0]
    st_ref[...] = c_ref[t - 2:t, :] * u_ref[t - 2:t, :]
    short_ref[...] = b_ref[...] * _conv_rows(c_ref[...] * u_ref[...], w_ref)


def _convgate_prompt(pg, conv_w, n_seq, t, dc):
    tc = _tile(dc, 256)
    nj = dc // tc
    return pl.pallas_call(
        _convgate_p_kernel,
        grid=(n_seq, nj),
        in_specs=[
            pl.BlockSpec((t, tc), lambda n, j: (n, j)),
            pl.BlockSpec((t, tc), lambda n, j: (n, j + nj)),
            pl.BlockSpec((t, tc), lambda n, j: (n, j + 2 * nj)),
            pl.BlockSpec((CONV_WIDTH, tc), lambda n, j: (0, j)),
        ],
        out_specs=[pl.BlockSpec((t, tc), lambda n, j: (n, j)), pl.BlockSpec((None, 2, tc), lambda n, j: (n, 0, j))],
        out_shape=[jax.ShapeDtypeStruct((n_seq * t, dc), F32), jax.ShapeDtypeStruct((n_seq, 2, dc), F32)],
        compiler_params=_params(2),
        name="convgate_prompt",
    )(pg, pg, pg, conv_w)


FC1_ROW_CHUNKS = 4


def _conv_rows_after(u, tail, w_ref):
    row = lax.broadcasted_iota(jnp.int32, u.shape, 0)
    p1, p2 = tail[7:8, :], tail[6:7, :]
    s1 = jnp.where(row == 0, p1, pltpu.roll(u, 1, 0))
    s2 = jnp.where(row == 0, p2, jnp.where(row == 1, p1, pltpu.roll(u, 2, 0)))
    return w_ref[0:1, :] * s2 + w_ref[1:2, :] * s1 + w_ref[2:3, :] * u


def _fc1_gate_kernel(h_ref, wa_ref, wb_ref, cw_ref, *rest, prev_steps):
    wa = wa_ref[...].astype(BF16)
    wb = wb_ref[...].astype(BF16)
    if prev_steps == 0:
        g_ref, st_ref = rest
        t = h_ref.shape[0]
        n_chunks = FC1_ROW_CHUNKS if t % (8 * FC1_ROW_CHUNKS) == 0 else 1
        rc = t // n_chunks

        def dots(r):
            h = h_ref[r * rc:(r + 1) * rc, :]
            return jnp.dot(h, wa, preferred_element_type=F32), jnp.dot(h, wb, preferred_element_type=F32)

        def gate(r, ab, tail):
            a, b = ab
            g_ref[r * rc:(r + 1) * rc, :] = (_silu(_conv_rows_after(a, tail, cw_ref)) * b).astype(g_ref.dtype)

        tail = jnp.zeros((8, wa.shape[1]), F32)
        pending = dots(0)
        for r in range(1, n_chunks):
            nxt = dots(r)
            gate(r - 1, pending, tail)
            tail = pending[0][rc - 8:rc, :]
            pending = nxt
        gate(n_chunks - 1, pending, tail)
        st_ref[...] = pending[0][rc - 8:rc, :]
    else:
        prev_ref, g_ref, st_ref = rest
        h = h_ref[...]
        a = jnp.dot(h, wa, preferred_element_type=F32)
        b = jnp.dot(h, wb, preferred_element_type=F32)
        ns = prev_ref.shape[1]
        ys, last = _conv_time_major([a[t * ns:(t + 1) * ns, :] for t in range(prev_steps)], prev_ref, cw_ref)
        for t in range(prev_steps):
            g_ref[t * ns:(t + 1) * ns, :] = (_silu(ys[t]) * b[t * ns:(t + 1) * ns, :]).astype(g_ref.dtype)
        st_ref[0] = last[0]
        st_ref[1] = last[1]


def _fc1_gate(h, w_fc1, conv_w, dff, *, tm, prev=None):
    m, d = h.shape
    tn = _tile(dff, 256)
    nj = dff // tn
    in_specs = [
        pl.BlockSpec((tm, d), lambda i, j: (i, 0), pipeline_mode=pl.Buffered(1)),
        pl.BlockSpec((d, tn), lambda i, j: (0, j)),
        pl.BlockSpec((d, tn), lambda i, j: (0, j + nj)),
        pl.BlockSpec((CONV_WIDTH, tn), lambda i, j: (0, j)),
    ]
    args = [h, w_fc1, w_fc1, conv_w]
    if prev is None:
        prev_steps = 0
        st_spec = pl.BlockSpec((None, 8, tn), lambda i, j: (i, 0, j))
        st_shape = jax.ShapeDtypeStruct((m // tm, 8, dff), F32)
    else:
        ns = prev.shape[1]
        prev_steps = m // ns
        assert tm == m
        in_specs.append(pl.BlockSpec((2, ns, tn), lambda i, j: (0, 0, j)))
        args.append(prev)
        st_spec = pl.BlockSpec((2, ns, tn), lambda i, j: (0, 0, j))
        st_shape = jax.ShapeDtypeStruct((2, ns, dff), F32)
    return pl.pallas_call(
        functools.partial(_fc1_gate_kernel, prev_steps=prev_steps),
        grid=(m // tm, nj),
        in_specs=in_specs,
        out_specs=[pl.BlockSpec((tm, tn), lambda i, j: (i, j)), st_spec],
        out_shape=[jax.ShapeDtypeStruct((m, dff), BF16), st_shape],
        compiler_params=_params(2),
        name="fc1_gate",
    )(*args)


def _convgate_s_kernel(b_ref, c_ref, u_ref, prev_ref, w_ref, short_ref, st_ref, *, nt):
    ns = prev_ref.shape[1]
    cu = [c_ref[t * ns:(t + 1) * ns, :] * u_ref[t * ns:(t + 1) * ns, :] for t in range(nt)]
    ys, last = _conv_time_major(cu, prev_ref, w_ref)
    for t in range(nt):
        short_ref[t * ns:(t + 1) * ns, :] = b_ref[t * ns:(t + 1) * ns, :] * ys[t]
    st_ref[0] = last[0]
    st_ref[1] = last[1]


def _convgate_sample(pg, prev, conv_w, nt, ns, dc):
    tc = _tile(dc, 512)
    nj = dc // tc
    rows = nt * ns
    return pl.pallas_call(
        functools.partial(_convgate_s_kernel, nt=nt),
        grid=(nj,),
        in_specs=[
            pl.BlockSpec((rows, tc), lambda j: (0, j)),
            pl.BlockSpec((rows, tc), lambda j: (0, j + nj)),
            pl.BlockSpec((rows, tc), lambda j: (0, j + 2 * nj)),
            pl.BlockSpec((2, ns, tc), lambda j: (0, 0, j)),
            pl.BlockSpec((CONV_WIDTH, tc), lambda j: (0, j)),
        ],
        out_specs=[pl.BlockSpec((rows, tc), lambda j: (0, j)), pl.BlockSpec((2, ns, tc), lambda j: (0, 0, j))],
        out_shape=[jax.ShapeDtypeStruct((rows, dc), F32), jax.ShapeDtypeStruct((2, ns, dc), F32)],
        compiler_params=_params(1),
        name="convgate_sample",
    )(pg, pg, pg, prev, conv_w)


MOD_SH1, MOD_SC1, MOD_GT1, MOD_SH2, MOD_SC2, MOD_GT2 = range(6)


def _layer(x, rows, attend, convgate, ffn_gate, wts, da, dc, dff, q_scale=None):
    h = _prenorm(rows, x, wts["g_pre_mix"], MOD_SH1, MOD_SC1)
    q, k, k_att, v, v_att, pg = _proj_all(h, wts["w_in"], da, dc, q_scale)
    if q_scale is None:
        k_att, v_att = k, v
    att = attend(q, k_att, v_att)
    short, conv_state = convgate(pg)
    merged = _merge_norm(rows, att, short, wts["g_attn_out"], wts["g_conv_out"])
    o = _matmul(merged, wts["w_o"], name="out_proj")
    x1, h2 = _resid_prenorm(rows, x, o, wts["g_post_mix"], MOD_GT1, wts["g_pre_ffn"], MOD_SH2, MOD_SC2)
    g, ffn_state = ffn_gate(h2)
    f = _matmul(g, wts["w_fc2"], tm_pref=512, tk_pref=g.shape[1], name="fc2")
    y = _resid(rows, x1, f, wts["g_post_ffn"], MOD_GT2)
    return y, k, v, conv_state, ffn_state


def kernel(x_prompt, x_sample, c_prompt, c_sample, cache_k, cache_v, state_conv, state_ffn_conv, page_table, w_ada, b_ada, g_pre_mix, g_post_mix, w_in, sb_bias, conv_w, g_attn_out, g_conv_out, w_o, g_pre_ffn, g_post_ffn, w_fc1, ffn_conv_w, w_fc2):
    b, t, d = x_prompt.shape
    ns, nt, _ = x_sample.shape
    depth, n_phys, page, nh, dh = cache_k.shape
    assert depth == 1 and conv_w.shape[1] == CONV_WIDTH and ffn_conv_w.shape[1] == CONV_WIDTH
    da, dc, dff = nh * dh, conv_w.shape[-1], ffn_conv_w.shape[-1]

    wts = {
        "w_in": w_in.reshape(d, -1),
        "w_o": w_o.reshape(da + dc, d),
        "w_fc1": w_fc1.reshape(d, 2 * dff),
        "w_fc2": w_fc2.reshape(dff, d).astype(BF16),
        "g_pre_mix": g_pre_mix.reshape(1, d), "g_post_mix": g_post_mix.reshape(1, d),
        "g_attn_out": g_attn_out.reshape(1, da), "g_conv_out": g_conv_out.reshape(1, dc),
        "g_pre_ffn": g_pre_ffn.reshape(1, d), "g_post_ffn": g_post_ffn.reshape(1, d),
    }
    cw, fcw = conv_w.reshape(CONV_WIDTH, dc), ffn_conv_w.reshape(CONV_WIDTH, dff)
    bias = sb_bias.reshape(nh)

    n_c = ns + b
    pad = -n_c % 16
    c_all = jnp.concatenate([c_sample, c_prompt, jnp.zeros((pad, d), F32)], axis=0)
    mod = _ada(c_all, w_ada.reshape(d, 6 * d), b_ada.reshape(1, 6 * d))
    mod_s = mod[:ns].reshape(1, ns, 6 * d)
    mod_p = mod[ns:n_c].reshape(b, 1, 6 * d)

    tm_p = _tile(t, 256)
    rows_p = _Rows(b * t, tm_p, t // tm_p, mod_p)

    def attend_p(q, k, v):
        r3 = lambda a: a.reshape(b, t, da)
        return _prompt_attention(r3(q), r3(k), r3(v), bias, nh, dh).reshape(b * t, da)

    yp, kp, vp, cp, fp = _layer(
        x_prompt.reshape(b * t, d), rows_p, attend_p,
        lambda pg: _convgate_prompt(pg, cw, b, t, dc),
        lambda h2: _fc1_gate(h2, wts["w_fc1"], fcw, dff, tm=t),
        wts, da, dc, dff, q_scale=LOG2E * dh ** -0.5)

    rows_s = _Rows(nt * ns, ns, nt, mod_s)
    to_seq_major = lambda a: a.reshape(nt, ns, -1).transpose(1, 0, 2)
    to_time_major = lambda a: a.transpose(1, 0, 2).reshape(nt * ns, -1)
    kv_seq = {}

    def attend_s(q, k, v):
        kv_seq["k"], kv_seq["v"] = to_seq_major(k), to_seq_major(v)
        att = _sample_attention(to_seq_major(q), kv_seq["k"], kv_seq["v"],
                                cache_k.reshape(n_phys, page, nh, dh), cache_v.reshape(n_phys, page, nh, dh),
                                page_table, bias, nh, dh)
        return to_time_major(att)

    prev_conv = state_conv.reshape(ns, CONV_WIDTH - 1, dc).transpose(1, 0, 2)
    prev_ffn = state_ffn_conv.reshape(ns, CONV_WIDTH - 1, dff).transpose(1, 0, 2)
    ys, _, _, cs, fs = _layer(
        to_time_major(x_sample), rows_s, attend_s,
        lambda pg: _convgate_sample(pg, prev_conv, cw, nt, ns, dc),
        lambda h2: _fc1_gate(h2, wts["w_fc1"], fcw, dff, tm=nt * ns, prev=prev_ffn),
        wts, da, dc, dff)

    return (
        yp.reshape(b, t, d),
        to_seq_major(ys),
        kp.reshape(1, b, t, nh, dh),
        vp.reshape(1, b, t, nh, dh),
        cp.reshape(1, b, CONV_WIDTH - 1, dc),
        fp[:, 8 - (CONV_WIDTH - 1):, :].reshape(1, b, CONV_WIDTH - 1, dff),
        kv_seq["k"].reshape(1, ns, nt, nh, dh),
        kv_seq["v"].reshape(1, ns, nt, nh, dh),
        cs.transpose(1, 0, 2).reshape(1, ns, CONV_WIDTH - 1, dc),
        fs.transpose(1, 0, 2).reshape(1, ns, CONV_WIDTH - 1, dff),
    )
```

```python
import functools

import jax
import jax.numpy as jnp
from jax import lax
from jax.experimental import pallas as pl
from jax.experimental.pallas import tpu as pltpu

F32 = jnp.float32
BF16 = jnp.bfloat16
EPS = 1e-6
CONV_WIDTH = 3
V7X_VMEM_LIMIT_BYTES = 56 * 1024 * 1024
LANE = 128
NT_DIMS = (((1,), (1,)), ((), ()))


def _params(n_axes, vmem=V7X_VMEM_LIMIT_BYTES):
    return pltpu.CompilerParams(dimension_semantics=("arbitrary",) * n_axes, vmem_limit_bytes=vmem)


def _tile(dim, pref):
    if dim <= pref:
        return dim
    t = (pref // LANE) * LANE
    while dim % t:
        t -= LANE
    return t


def _rms(x):
    return x * lax.rsqrt(jnp.mean(x * x, axis=-1, keepdims=True) + EPS)


def _silu(x):
    return x * jax.nn.sigmoid(x)


def _ada_kernel(c_ref, w_ref, b_ref, o_ref, a_ref):
    @pl.when(pl.program_id(0) == 0)
    def _():
        a_ref[...] = _silu(c_ref[...]).astype(BF16)

    o_ref[...] = jnp.dot(a_ref[...], w_ref[...].astype(BF16), preferred_element_type=F32) + b_ref[...]


def _ada(c_all, w, b):
    rows, d = c_all.shape
    n = w.shape[1]
    tn = _tile(n, 1024)
    return pl.pallas_call(
        _ada_kernel,
        grid=(n // tn,),
        in_specs=[
            pl.BlockSpec((rows, d), lambda j: (0, 0)),
            pl.BlockSpec((d, tn), lambda j: (0, j)),
            pl.BlockSpec((1, tn), lambda j: (0, j)),
        ],
        out_specs=pl.BlockSpec((rows, tn), lambda j: (0, j)),
        out_shape=jax.ShapeDtypeStruct((rows, n), F32),
        scratch_shapes=[pltpu.VMEM((rows, d), BF16)],
        compiler_params=_params(1),
        name="ada_mod",
    )(c_all, w, b)


class _Rows:
    def __init__(self, n_rows, tm, tiles_per_group, mod):
        self.n_rows, self.tm, self.tpg, self.mod = n_rows, tm, tiles_per_group, mod
        self.grid = (n_rows // tm,)

    def row(self, width, col=0):
        return pl.BlockSpec((self.tm, width), lambda i: (i, col))

    def vec(self, width):
        return pl.BlockSpec((1, width), lambda i: (0, 0))

    def modv(self, chunk, width):
        tpg = self.tpg
        return pl.BlockSpec((None, self.mod.shape[1], width), lambda i: (i // tpg, 0, chunk))


def _prenorm_kernel(x_ref, g_ref, sh_ref, sc_ref, h_ref):
    y = _rms(x_ref[...]) * g_ref[...]
    h_ref[...] = (y * (1.0 + sc_ref[...]) + sh_ref[...]).astype(h_ref.dtype)


def _prenorm(rows, x, g, sh_chunk, sc_chunk):
    d = x.shape[1]
    return pl.pallas_call(
        _prenorm_kernel,
        grid=rows.grid,
        in_specs=[rows.row(d), rows.vec(d), rows.modv(sh_chunk, d), rows.modv(sc_chunk, d)],
        out_specs=rows.row(d),
        out_shape=jax.ShapeDtypeStruct(x.shape, BF16),
        compiler_params=_params(1),
        name="prenorm",
    )(x, g, rows.mod, rows.mod)


def _merge_norm_kernel(att_ref, short_ref, ga_ref, gc_ref, o_ref):
    da = att_ref.shape[1]
    o_ref[:, :da] = (_rms(att_ref[...]) * ga_ref[...]).astype(o_ref.dtype)
    o_ref[:, da:] = (_rms(short_ref[...]) * gc_ref[...]).astype(o_ref.dtype)


def _merge_norm(rows, att, short, ga, gc):
    da, dc = att.shape[1], short.shape[1]
    return pl.pallas_call(
        _merge_norm_kernel,
        grid=rows.grid,
        in_specs=[rows.row(da), rows.row(dc), rows.vec(da), rows.vec(dc)],
        out_specs=rows.row(da + dc),
        out_shape=jax.ShapeDtypeStruct((att.shape[0], da + dc), BF16),
        compiler_params=_params(1),
        name="merge_norm",
    )(att, short, ga, gc)


def _resid_prenorm_kernel(x_ref, o_ref, gpost_ref, gt_ref, gpre_ref, sh_ref, sc_ref, x1_ref, h_ref):
    x1 = x_ref[...] + gt_ref[...] * (_rms(o_ref[...]) * gpost_ref[...])
    x1_ref[...] = x1
    y = _rms(x1) * gpre_ref[...]
    h_ref[...] = (y * (1.0 + sc_ref[...]) + sh_ref[...]).astype(h_ref.dtype)


def _resid_prenorm(rows, x, o, gpost, gt_chunk, gpre, sh_chunk, sc_chunk):
    d = x.shape[1]
    return pl.pallas_call(
        _resid_prenorm_kernel,
        grid=rows.grid,
        in_specs=[rows.row(d), rows.row(d), rows.vec(d), rows.modv(gt_chunk, d), rows.vec(d),
                  rows.modv(sh_chunk, d), rows.modv(sc_chunk, d)],
        out_specs=[rows.row(d), rows.row(d)],
        out_shape=[jax.ShapeDtypeStruct(x.shape, F32), jax.ShapeDtypeStruct(x.shape, BF16)],
        compiler_params=_params(1),
        name="resid_prenorm",
    )(x, o, gpost, rows.mod, gpre, rows.mod, rows.mod)


def _resid_kernel(x_ref, f_ref, gpost_ref, gt_ref, y_ref):
    y_ref[...] = x_ref[...] + gt_ref[...] * (_rms(f_ref[...]) * gpost_ref[...])


def _resid(rows, x, f, gpost, gt_chunk):
    d = x.shape[1]
    return pl.pallas_call(
        _resid_kernel,
        grid=rows.grid,
        in_specs=[rows.row(d), rows.row(d), rows.vec(d), rows.modv(gt_chunk, d)],
        out_specs=rows.row(d),
        out_shape=jax.ShapeDtypeStruct(x.shape, F32),
        compiler_params=_params(1),
        name="resid",
    )(x, f, gpost, rows.mod)


def _mm_kernel(a_ref, w_ref, *o_refs, nk, scale):
    acc = jnp.dot(a_ref[...], w_ref[...].astype(BF16), preferred_element_type=F32)
    if nk == 1:
        if scale is not None:
            acc = acc * scale
        for o_ref in o_refs:
            o_ref[...] = acc.astype(o_ref.dtype)
    else:
        o_ref, = o_refs
        k = pl.program_id(2)

        @pl.when(k == 0)
        def _():
            o_ref[...] = acc

        @pl.when(k != 0)
        def _():
            o_ref[...] += acc


def _matmul(a, w, *, col0=0, ncols=None, out_dtypes=(F32,), scale=None, tm_pref=1024, tn_pref=512, tk_pref=5632,
            name="matmul"):
    m, kdim = a.shape
    ncols = w.shape[1] if ncols is None else ncols
    tm, tn, tk = _tile(m, tm_pref), _tile(ncols, tn_pref), _tile(kdim, tk_pref)
    assert col0 % tn == 0
    cb, nk = col0 // tn, kdim // tk
    assert nk == 1 or (tuple(out_dtypes) == (F32,) and scale is None)
    outs = pl.pallas_call(
        functools.partial(_mm_kernel, nk=nk, scale=scale),
        grid=(m // tm, ncols // tn, nk),
        in_specs=[
            pl.BlockSpec((tm, tk), lambda i, j, k: (i, k)),
            pl.BlockSpec((tk, tn), lambda i, j, k: (k, j + cb)),
        ],
        out_specs=[pl.BlockSpec((tm, tn), lambda i, j, k: (i, j)) for _ in out_dtypes],
        out_shape=[jax.ShapeDtypeStruct((m, ncols), dt) for dt in out_dtypes],
        compiler_params=_params(3),
        name=name,
    )(a, w)
    return outs[0] if len(out_dtypes) == 1 else outs


def _proj_kernel(a_ref, w_ref, q_ref, k_ref, k16_ref, v_ref, v16_ref, g_ref, *, nq, q_scale):
    j = pl.program_id(1)

    def tile():
        return jnp.dot(a_ref[...], w_ref[...].astype(BF16), preferred_element_type=F32)

    @pl.when(j < nq)
    def _():
        acc = tile()
        q_ref[...] = (acc if q_scale is None else acc * q_scale).astype(q_ref.dtype)

    @pl.when(jnp.logical_and(j >= nq, j < 2 * nq))
    def _():
        acc = tile()
        k_ref[...] = acc
        k16_ref[...] = acc.astype(k16_ref.dtype)

    @pl.when(jnp.logical_and(j >= 2 * nq, j < 3 * nq))
    def _():
        acc = tile()
        v_ref[...] = acc
        v16_ref[...] = acc.astype(v16_ref.dtype)

    @pl.when(j >= 3 * nq)
    def _():
        g_ref[...] = tile()


def _proj_all(h, w_in, da, dc, q_scale):
    m, d = h.shape
    tm, tn = _tile(m, 2048), _tile(da, 256)
    assert da % tn == 0 and (3 * dc) % tn == 0
    nq, ng = da // tn, 3 * dc // tn

    def seg(first, count):
        return pl.BlockSpec((tm, tn), lambda i, j: (i, jnp.clip(j - first, 0, count - 1)))

    return pl.pallas_call(
        functools.partial(_proj_kernel, nq=nq, q_scale=q_scale),
        grid=(m // tm, 3 * nq + ng),
        in_specs=[
            pl.BlockSpec((tm, d), lambda i, j: (i, 0), pipeline_mode=pl.Buffered(1)),
            pl.BlockSpec((d, tn), lambda i, j: (0, j)),
        ],
        out_specs=[seg(0, nq), seg(nq, nq), seg(nq, nq), seg(2 * nq, nq), seg(2 * nq, nq), seg(3 * nq, ng)],
        out_shape=[jax.ShapeDtypeStruct((m, da), F32 if q_scale is None else BF16), jax.ShapeDtypeStruct((m, da), F32),
                   jax.ShapeDtypeStruct((m, da), BF16), jax.ShapeDtypeStruct((m, da), F32),
                   jax.ShapeDtypeStruct((m, da), BF16), jax.ShapeDtypeStruct((m, 3 * dc), F32)],
        compiler_params=_params(2),
        name="proj_all",
    )(h, w_in)


LOG2E = 1.4426950408889634


def _stick_chains(chains, u2):
    items = [(ci, bi) for ci, ch in enumerate(chains) for bi in range(len(ch[1]))]
    y, sp, drop, cum, a = {}, {}, {}, {}, {}
    for ci, bi in items:
        q, ks, _, bias, _, _, _ = chains[ci]
        y[ci, bi] = lax.dot_general(q, ks[bi], NT_DIMS, preferred_element_type=F32) + bias
    for it in items:
        sp[it] = jnp.maximum(y[it], 0.0) + jnp.log2(1.0 + jnp.exp2(-jnp.abs(y[it])))
        valid = chains[it[0]][6][it[1]]
        drop[it] = sp[it] if valid is None else jnp.where(valid, sp[it], 0.0)
    for it in items:
        hi = drop[it].astype(BF16)
        lo = (drop[it] - hi.astype(F32)).astype(BF16)
        cum[it] = jnp.dot(jnp.concatenate([hi, lo], axis=1), u2, preferred_element_type=F32)
    ds = [ch[4] for ch in chains]
    for it in items:
        ci = it[0]
        valid = chains[ci][6][it[1]]
        e = jnp.exp2((y[it] - sp[it]) - (cum[it] + ds[ci]))
        a[it] = (e if valid is None else jnp.where(valid, e, 0.0)).astype(BF16)
        ds[ci] = ds[ci] + jnp.sum(drop[it], axis=-1, keepdims=True)
    accs = [ch[5] for ch in chains]
    for ci, bi in items:
        accs[ci] = accs[ci] + jnp.dot(a[ci, bi], chains[ci][2][bi], preferred_element_type=F32)
    return list(zip(ds, accs))


def _strict_lower_twice(n):
    r = lax.broadcasted_iota(jnp.int32, (n, n), 0)
    c = lax.broadcasted_iota(jnp.int32, (n, n), 1)
    lower = (r > c).astype(BF16)
    return jnp.concatenate([lower, lower], axis=0)


PATTN_HEADS_PER_STEP = 8


def _pattn_kernel(bias_ref, q_ref, k_ref, v_ref, u_ref, o_ref, *, blk, dh, hps):
    hg, qi = pl.program_id(1), pl.program_id(2)
    u2 = u_ref[...]
    heads = [slice(i * dh, (i + 1) * dh) for i in range(hps)]
    qs = [q_ref[:, hd] for hd in heads]
    biases = [LOG2E * bias_ref[hg * hps + i] for i in range(hps)]

    def visit(kbs, state, valids):
        starts = [pl.multiple_of(kb * blk, blk) for kb in kbs]
        chains = [(qs[i], [k_ref[pl.ds(st, blk), heads[i]] for st in starts],
                   [v_ref[pl.ds(st, blk), heads[i]] for st in starts],
                   biases[i], state[i][0], state[i][1], valids) for i in range(hps)]
        return _stick_chains(chains, u2)

    row = lax.broadcasted_iota(jnp.int32, (blk, blk), 0)
    col = lax.broadcasted_iota(jnp.int32, (blk, blk), 1)
    diag = col < row
    state = [(jnp.zeros((blk, 1), F32), jnp.zeros((blk, dh), F32)) for _ in range(hps)]
    odd = qi & 1
    state = lax.cond(odd == 1, lambda: visit([qi, qi - 1], state, [diag, None]), lambda: visit([qi], state, [diag]))
    nxt = qi - 1 - odd
    state = lax.fori_loop(0, lax.shift_right_logical(qi, 1),
                          lambda it, st: visit([nxt - 2 * it, nxt - 2 * it - 1], st, [None, None]), state)
    for i in range(hps):
        o_ref[:, heads[i]] = state[i][1]


def _prompt_attention(q, k, v, bias, nh, dh):
    b, t, da = q.shape
    blk = _tile(t, 256)
    hps = min(PATTN_HEADS_PER_STEP, nh)
    assert nh % hps == 0
    u2 = _strict_lower_twice(blk)
    qspec = pl.BlockSpec((None, blk, hps * dh), lambda n, h, i: (n, i, h))
    kvspec = pl.BlockSpec((None, t, hps * dh), lambda n, h, i: (n, 0, h))
    return pl.pallas_call(
        functools.partial(_pattn_kernel, blk=blk, dh=dh, hps=hps),
        grid=(b, nh // hps, t // blk),
        in_specs=[
            pl.BlockSpec(memory_space=pltpu.SMEM),
            qspec, kvspec, kvspec,
            pl.BlockSpec((2 * blk, blk), lambda n, h, i: (0, 0)),
        ],
        out_specs=qspec,
        out_shape=jax.ShapeDtypeStruct((b, t, da), F32),
        compiler_params=_params(3),
        name="prompt_attn",
    )(bias, q, k, v, u2)


SATTN_PAGES_PER_STEP = 4
SATTN_HEADS_PER_BLOCK = 16


def _sattn_kernel(pt_ref, q_ref, kn_ref, vn_ref, *rest, nh, dh, nt, page, ppb, hpb, scale):
    nhb = nh // hpb
    n_ops = ppb * nhb
    k_refs, v_refs = rest[:n_ops], rest[n_ops:2 * n_ops]
    bias_ref, u_ref, o_ref, qbd_ref, d_ref, acc_ref, kpad_ref, vpad_ref, kbuf_ref, vbuf_ref = rest[2 * n_ops:]
    p = pl.program_id(1)
    da = nh * dh
    nrows = nt * nh
    log_nh, log_dh = nh.bit_length() - 1, dh.bit_length() - 1
    r = lax.broadcasted_iota(jnp.int32, (nrows, da), 0)
    lane = lax.broadcasted_iota(jnp.int32, (nrows, da), 1)
    own_head = (lane >> log_dh) == (r & (nh - 1))

    @pl.when(p == 0)
    def _():
        qrep = jnp.concatenate([jnp.broadcast_to(q_ref[t:t + 1, :], (nh, da)) for t in range(nt)], axis=0)
        qbd_ref[...] = jnp.where(own_head, qrep * (LOG2E * scale), 0.0).astype(BF16)
        kpad_ref[...] = jnp.zeros(kpad_ref.shape, F32)
        vpad_ref[...] = jnp.zeros(vpad_ref.shape, F32)
        kpad_ref[0:nt, :] = kn_ref[...]
        vpad_ref[0:nt, :] = vn_ref[...]
        rr = lax.broadcasted_iota(jnp.int32, (nrows, page), 0)
        cc = lax.broadcasted_iota(jnp.int32, (nrows, page), 1)
        valid = cc < (rr >> log_nh)
        u2_new = jnp.concatenate([u_ref[0:page, 0:page], u_ref[2 * page:3 * page, 0:page]], axis=0)
        (d, acc), = _stick_chains([(qbd_ref[...], [kpad_ref[...].astype(BF16)], [vpad_ref[...].astype(BF16)],
                                    bias_ref[:, 0:page], jnp.zeros((nrows, 1), F32), jnp.zeros((nrows, da), F32),
                                    [valid])], u2_new)
        d_ref[...] = d
        acc_ref[...] = acc

    def regroup(src_refs, dst_ref):
        for j in range(ppb):
            for hb in range(nhb):
                by_head = jnp.swapaxes(src_refs[j * nhb + hb][...].astype(BF16), 0, 1)
                for s in range(hpb):
                    head = hb * hpb + s
                    dst_ref[j * page:(j + 1) * page, head * dh:(head + 1) * dh] = by_head[s]

    regroup(k_refs, kbuf_ref)
    regroup(v_refs, vbuf_ref)

    chunk = 2 * page
    order = list(reversed(range(ppb * page // chunk)))
    (d, acc), = _stick_chains([(qbd_ref[...], [kbuf_ref[ci * chunk:(ci + 1) * chunk, :] for ci in order],
                                [vbuf_ref[ci * chunk:(ci + 1) * chunk, :] for ci in order],
                                bias_ref[...], d_ref[...], acc_ref[...], [None] * len(order))], u_ref[...])
    d_ref[...] = d
    acc_ref[...] = acc

    @pl.when(p == pl.num_programs(1) - 1)
    def _():
        own = jnp.where(own_head, acc, 0.0)
        o_ref[...] = jnp.sum(own.reshape(nt, nh, da), axis=1)


def _sample_attention(q, k_new, v_new, k_pages, v_pages, page_table, bias, nh, dh):
    ns, nt, da = q.shape
    page = k_pages.shape[1]
    npg = page_table.shape[1]
    ppb = min(SATTN_PAGES_PER_STEP, npg)
    hpb = min(SATTN_HEADS_PER_BLOCK, nh)
    assert npg % ppb == 0 and ppb % 2 == 0 and nh % hpb == 0
    assert nh & (nh - 1) == 0 and dh & (dh - 1) == 0 and nt <= page
    nrows = nt * nh
    u2 = _strict_lower_twice(2 * page)
    bias_rows = jnp.broadcast_to(jnp.tile(LOG2E * bias.astype(F32), nt)[:, None], (nrows, 2 * page))
    pt = page_table.reshape(-1).astype(jnp.int32)

    seq = pl.BlockSpec((None, nt, da), lambda n, p, pt: (n, 0, 0))

    def page_spec(j, hb):
        return pl.BlockSpec((None, page, hpb, dh),
                            lambda n, p, pt: (pt[n * npg + npg - ppb * (p + 1) + j], 0, hb, 0))

    page_specs = [page_spec(j, hb) for j in range(ppb) for hb in range(nh // hpb)]
    const = lambda shape: pl.BlockSpec(shape, lambda n, p, pt: (0, 0))
    grid_spec = pltpu.PrefetchScalarGridSpec(
        num_scalar_prefetch=1,
        grid=(ns, npg // ppb),
        in_specs=[seq, seq, seq] + page_specs + page_specs + [const((nrows, 2 * page)), const((4 * page, 2 * page))],
        out_specs=seq,
        scratch_shapes=[
            pltpu.VMEM((nrows, da), BF16),
            pltpu.VMEM((nrows, 1), F32),
            pltpu.VMEM((nrows, da), F32),
            pltpu.VMEM((page, da), F32),
            pltpu.VMEM((page, da), F32),
            pltpu.VMEM((ppb * page, da), BF16),
            pltpu.VMEM((ppb * page, da), BF16),
        ],
    )
    n_ops = len(page_specs)
    return pl.pallas_call(
        functools.partial(_sattn_kernel, nh=nh, dh=dh, nt=nt, page=page, ppb=ppb, hpb=hpb, scale=dh ** -0.5),
        grid_spec=grid_spec,
        out_shape=jax.ShapeDtypeStruct((ns, nt, da), F32),
        compiler_params=_params(2),
        name="sample_attn",
    )(pt, q, k_new, v_new, *([k_pages] * n_ops), *([v_pages] * n_ops), bias_rows, u2)


def _shift_rows(x, s):
    row = lax.broadcasted_iota(jnp.int32, x.shape, 0)
    return jnp.where(row >= s, pltpu.roll(x, s, 0), 0.0)


def _conv_rows(u, w_ref):
    return w_ref[0:1, :] * _shift_rows(u, 2) + w_ref[1:2, :] * _shift_rows(u, 1) + w_ref[2:3, :] * u


def _conv_time_major(us, prev_ref, w_ref):
    seq = [prev_ref[0], prev_ref[1]] + us
    ys = [w_ref[0:1, :] * seq[t] + w_ref[1:2, :] * seq[t + 1] + w_ref[2:3, :] * seq[t + 2] for t in range(len(us))]
    return ys, seq[-2:]


def _convgate_p_kernel(b_ref, c_ref, u_ref, w_ref, short_ref, st_ref):
    t = c_ref.shape[0]
    st_ref[...] = c_ref[t - 2:t, :] * u_ref[t - 2:t, :]
    short_ref[...] = b_ref[...] * _conv_rows(c_ref[...] * u_ref[...], w_ref)


def _convgate_prompt(pg, conv_w, n_seq, t, dc):
    tc = _tile(dc, 256)
    nj = dc // tc
    return pl.pallas_call(
        _convgate_p_kernel,
        grid=(n_seq, nj),
        in_specs=[
            pl.BlockSpec((t, tc), lambda n, j: (n, j)),
            pl.BlockSpec((t, tc), lambda n, j: (n, j + nj)),
            pl.BlockSpec((t, tc), lambda n, j: (n, j + 2 * nj)),
            pl.BlockSpec((CONV_WIDTH, tc), lambda n, j: (0, j)),
        ],
        out_specs=[pl.BlockSpec((t, tc), lambda n, j: (n, j)), pl.BlockSpec((None, 2, tc), lambda n, j: (n, 0, j))],
        out_shape=[jax.ShapeDtypeStruct((n_seq * t, dc), F32), jax.ShapeDtypeStruct((n_seq, 2, dc), F32)],
        compiler_params=_params(2),
        name="convgate_prompt",
    )(pg, pg, pg, conv_w)


FC1_ROW_CHUNKS = 4


def _conv_rows_after(u, tail, w_ref):
    row = lax.broadcasted_iota(jnp.int32, u.shape, 0)
    p1, p2 = tail[7:8, :], tail[6:7, :]
    s1 = jnp.where(row == 0, p1, pltpu.roll(u, 1, 0))
    s2 = jnp.where(row == 0, p2, jnp.where(row == 1, p1, pltpu.roll(u, 2, 0)))
    return w_ref[0:1, :] * s2 + w_ref[1:2, :] * s1 + w_ref[2:3, :] * u


def _fc1_gate_kernel(h_ref, wa_ref, wb_ref, cw_ref, *rest, prev_steps):
    wa = wa_ref[...].astype(BF16)
    wb = wb_ref[...].astype(BF16)
    if prev_steps == 0:
        g_ref, st_ref = rest
        t = h_ref.shape[0]
        n_chunks = FC1_ROW_CHUNKS if t % (8 * FC1_ROW_CHUNKS) == 0 else 1
        rc = t // n_chunks

        def dots(r):
            h = h_ref[r * rc:(r + 1) * rc, :]
            return jnp.dot(h, wa, preferred_element_type=F32), jnp.dot(h, wb, preferred_element_type=F32)

        def gate(r, ab, tail):
            a, b = ab
            g_ref[r * rc:(r + 1) * rc, :] = (_silu(_conv_rows_after(a, tail, cw_ref)) * b).astype(g_ref.dtype)

        tail = jnp.zeros((8, wa.shape[1]), F32)
        pending = dots(0)
        for r in range(1, n_chunks):
            nxt = dots(r)
            gate(r - 1, pending, tail)
            tail = pending[0][rc - 8:rc, :]
            pending = nxt
        gate(n_chunks - 1, pending, tail)
        st_ref[...] = pending[0][rc - 8:rc, :]
    else:
        prev_ref, g_ref, st_ref = rest
        h = h_ref[...]
        a = jnp.dot(h, wa, preferred_element_type=F32)
        b = jnp.dot(h, wb, preferred_element_type=F32)
        ns = prev_ref.shape[1]
        ys, last = _conv_time_major([a[t * ns:(t + 1) * ns, :] for t in range(prev_steps)], prev_ref, cw_ref)
        for t in range(prev_steps):
            g_ref[t * ns:(t + 1) * ns, :] = (_silu(ys[t]) * b[t * ns:(t + 1) * ns, :]).astype(g_ref.dtype)
        st_ref[0] = last[0]
        st_ref[1] = last[1]


def _fc1_gate(h, w_fc1, conv_w, dff, *, tm, prev=None):
    m, d = h.shape
    tn = _tile(dff, 256)
    nj = dff // tn
    in_specs = [
        pl.BlockSpec((tm, d), lambda i, j: (i, 0), pipeline_mode=pl.Buffered(1)),
        pl.BlockSpec((d, tn), lambda i, j: (0, j)),
        pl.BlockSpec((d, tn), lambda i, j: (0, j + nj)),
        pl.BlockSpec((CONV_WIDTH, tn), lambda i, j: (0, j)),
    ]
    args = [h, w_fc1, w_fc1, conv_w]
    if prev is None:
        prev_steps = 0
        st_spec = pl.BlockSpec((None, 8, tn), lambda i, j: (i, 0, j))
        st_shape = jax.ShapeDtypeStruct((m // tm, 8, dff), F32)
    else:
        ns = prev.shape[1]
        prev_steps = m // ns
        assert tm == m
        in_specs.append(pl.BlockSpec((2, ns, tn), lambda i, j: (0, 0, j)))
        args.append(prev)
        st_spec = pl.BlockSpec((2, ns, tn), lambda i, j: (0, 0, j))
        st_shape = jax.ShapeDtypeStruct((2, ns, dff), F32)
    return pl.pallas_call(
        functools.partial(_fc1_gate_kernel, prev_steps=prev_steps),
        grid=(m // tm, nj),
        in_specs=in_specs,
        out_specs=[pl.BlockSpec((tm, tn), lambda i, j: (i, j)), st_spec],
        out_shape=[jax.ShapeDtypeStruct((m, dff), BF16), st_shape],
        compiler_params=_params(2),
        name="fc1_gate",
    )(*args)


def _convgate_s_kernel(b_ref, c_ref, u_ref, prev_ref, w_ref, short_ref, st_ref, *, nt):
    ns = prev_ref.shape[1]
    cu = [c_ref[t * ns:(t + 1) * ns, :] * u_ref[t * ns:(t + 1) * ns, :] for t in range(nt)]
    ys, last = _conv_time_major(cu, prev_ref, w_ref)
    for t in range(nt):
        short_ref[t * ns:(t + 1) * ns, :] = b_ref[t * ns:(t + 1) * ns, :] * ys[t]
    st_ref[0] = last[0]
    st_ref[1] = last[1]


def _convgate_sample(pg, prev, conv_w, nt, ns, dc):
    tc = _tile(dc, 512)
    nj = dc // tc
    rows = nt * ns
    return pl.pallas_call(
        functools.partial(_convgate_s_kernel, nt=nt),
        grid=(nj,),
        in_specs=[
            pl.BlockSpec((rows, tc), lambda j: (0, j)),
            pl.BlockSpec((rows, tc), lambda j: (0, j + nj)),
            pl.BlockSpec((rows, tc), lambda j: (0, j + 2 * nj)),
            pl.BlockSpec((2, ns, tc), lambda j: (0, 0, j)),
            pl.BlockSpec((CONV_WIDTH, tc), lambda j: (0, j)),
        ],
        out_specs=[pl.BlockSpec((rows, tc), lambda j: (0, j)), pl.BlockSpec((2, ns, tc), lambda j: (0, 0, j))],
        out_shape=[jax.ShapeDtypeStruct((rows, dc), F32), jax.ShapeDtypeStruct((2, ns, dc), F32)],
        compiler_params=_params(1),
        name="convgate_sample",
    )(pg, pg, pg, prev, conv_w)


MOD_SH1, MOD_SC1, MOD_GT1, MOD_SH2, MOD_SC2, MOD_GT2 = range(6)


def _layer(x, rows, attend, convgate, ffn_gate, wts, da, dc, dff, q_scale=None):
    h = _prenorm(rows, x, wts["g_pre_mix"], MOD_SH1, MOD_SC1)
    q, k, k_att, v, v_att, pg = _proj_all(h, wts["w_in"], da, dc, q_scale)
    if q_scale is None:
        k_att, v_att = k, v
    att = attend(q, k_att, v_att)
    short, conv_state = convgate(pg)
    merged = _merge_norm(rows, att, short, wts["g_attn_out"], wts["g_conv_out"])
    o = _matmul(merged, wts["w_o"], name="out_proj")
    x1, h2 = _resid_prenorm(rows, x, o, wts["g_post_mix"], MOD_GT1, wts["g_pre_ffn"], MOD_SH2, MOD_SC2)
    g, ffn_state = ffn_gate(h2)
    f = _matmul(g, wts["w_fc2"], tm_pref=512, tk_pref=g.shape[1], name="fc2")
    y = _resid(rows, x1, f, wts["g_post_ffn"], MOD_GT2)
    return y, k, v, conv_state, ffn_state


def kernel(x_prompt, x_sample, c_prompt, c_sample, cache_k, cache_v, state_conv, state_ffn_conv, page_table, w_ada, b_ada, g_pre_mix, g_post_mix, w_in, sb_bias, conv_w, g_attn_out, g_conv_out, w_o, g_pre_ffn, g_post_ffn, w_fc1, ffn_conv_w, w_fc2):
    b, t, d = x_prompt.shape
    ns, nt, _ = x_sample.shape
    depth, n_phys, page, nh, dh = cache_k.shape
    assert depth == 1 and conv_w.shape[1] == CONV_WIDTH and ffn_conv_w.shape[1] == CONV_WIDTH
    da, dc, dff = nh * dh, conv_w.shape[-1], ffn_conv_w.shape[-1]

    wts = {
        "w_in": w_in.reshape(d, -1),
        "w_o": w_o.reshape(da + dc, d),
        "w_fc1": w_fc1.reshape(d, 2 * dff),
        "w_fc2": w_fc2.reshape(dff, d).astype(BF16),
        "g_pre_mix": g_pre_mix.reshape(1, d), "g_post_mix": g_post_mix.reshape(1, d),
        "g_attn_out": g_attn_out.reshape(1, da), "g_conv_out": g_conv_out.reshape(1, dc),
        "g_pre_ffn": g_pre_ffn.reshape(1, d), "g_post_ffn": g_post_ffn.reshape(1, d),
    }
    cw, fcw = conv_w.reshape(CONV_WIDTH, dc), ffn_conv_w.reshape(CONV_WIDTH, dff)
    bias = sb_bias.reshape(nh)

    n_c = ns + b
    pad = -n_c % 16
    c_all = jnp.concatenate([c_sample, c_prompt, jnp.zeros((pad, d), F32)], axis=0)
    mod = _ada(c_all, w_ada.reshape(d, 6 * d), b_ada.reshape(1, 6 * d))
    mod_s = mod[:ns].reshape(1, ns, 6 * d)
    mod_p = mod[ns:n_c].reshape(b, 1, 6 * d)

    tm_p = _tile(t, 256)
    rows_p = _Rows(b * t, tm_p, t // tm_p, mod_p)

    def attend_p(q, k, v):
        r3 = lambda a: a.reshape(b, t, da)
        return _prompt_attention(r3(q), r3(k), r3(v), bias, nh, dh).reshape(b * t, da)

    yp, kp, vp, cp, fp = _layer(
        x_prompt.reshape(b * t, d), rows_p, attend_p,
        lambda pg: _convgate_prompt(pg, cw, b, t, dc),
        lambda h2: _fc1_gate(h2, wts["w_fc1"], fcw, dff, tm=t),
        wts, da, dc, dff, q_scale=LOG2E * dh ** -0.5)

    rows_s = _Rows(nt * ns, ns, nt, mod_s)
    to_seq_major = lambda a: a.reshape(nt, ns, -1).transpose(1, 0, 2)
    to_time_major = lambda a: a.transpose(1, 0, 2).reshape(nt * ns, -1)
    kv_seq = {}

    def attend_s(q, k, v):
        kv_seq["k"], kv_seq["v"] = to_seq_major(k), to_seq_major(v)
        att = _sample_attention(to_seq_major(q), kv_seq["k"], kv_seq["v"],
                                cache_k.reshape(n_phys, page, nh, dh), cache_v.reshape(n_phys, page, nh, dh),
                                page_table, bias, nh, dh)
        return to_time_major(att)

    prev_conv = state_conv.reshape(ns, CONV_WIDTH - 1, dc).transpose(1, 0, 2)
    prev_ffn = state_ffn_conv.reshape(ns, CONV_WIDTH - 1, dff).transpose(1, 0, 2)
    ys, _, _, cs, fs = _layer(
        to_time_major(x_sample), rows_s, attend_s,
        lambda pg: _convgate_sample(pg, prev_conv, cw, nt, ns, dc),
        lambda h2: _fc1_gate(h2, wts["w_fc1"], fcw, dff, tm=nt * ns, prev=prev_ffn),
        wts, da, dc, dff)

    return (
        yp.reshape(b, t, d),
        to_seq_major(ys),
        kp.reshape(1, b, t, nh, dh),
        vp.reshape(1, b, t, nh, dh),
        cp.reshape(1, b, CONV_WIDTH - 1, dc),
        fp[:, 8 - (CONV_WIDTH - 1):, :].reshape(1, b, CONV_WIDTH - 1, dff),
        kv_seq["k"].reshape(1, ns, nt, nh, dh),
        kv_seq["v"].reshape(1, ns, nt, nh, dh),
        cs.transpose(1, 0, 2).reshape(1, ns, CONV_WIDTH - 1, dc),
        fs.transpose(1, 0, 2).reshape(1, ns, CONV_WIDTH - 1, dff),
    )
```

```python
import functools

import jax
import jax.numpy as jnp
from jax import lax
from jax.experimental import pallas as pl
from jax.experimental.pallas import tpu as pltpu

F32 = jnp.float32
BF16 = jnp.bfloat16
EPS = 1e-6
CONV_WIDTH = 3
V7X_VMEM_LIMIT_BYTES = 56 * 1024 * 1024
LANE = 128
NT_DIMS = (((1,), (1,)), ((), ()))


def _params(n_axes, vmem=V7X_VMEM_LIMIT_BYTES):
    return pltpu.CompilerParams(dimension_semantics=("arbitrary",) * n_axes, vmem_limit_bytes=vmem)


def _tile(dim, pref):
    if dim <= pref:
        return dim
    t = (pref // LANE) * LANE
    while dim % t:
        t -= LANE
    return t


def _rms(x):
    return x * lax.rsqrt(jnp.mean(x * x, axis=-1, keepdims=True) + EPS)


def _silu(x):
    return x * jax.nn.sigmoid(x)


def _ada_kernel(c_ref, w_ref, b_ref, o_ref, a_ref):
    @pl.when(pl.program_id(0) == 0)
    def _():
        a_ref[...] = _silu(c_ref[...]).astype(BF16)

    o_ref[...] = jnp.dot(a_ref[...], w_ref[...].astype(BF16), preferred_element_type=F32) + b_ref[...]


def _ada(c_all, w, b):
    rows, d = c_all.shape
    n = w.shape[1]
    tn = _tile(n, 1024)
    return pl.pallas_call(
        _ada_kernel,
        grid=(n // tn,),
        in_specs=[
            pl.BlockSpec((rows, d), lambda j: (0, 0)),
            pl.BlockSpec((d, tn), lambda j: (0, j)),
            pl.BlockSpec((1, tn), lambda j: (0, j)),
        ],
        out_specs=pl.BlockSpec((rows, tn), lambda j: (0, j)),
        out_shape=jax.ShapeDtypeStruct((rows, n), F32),
        scratch_shapes=[pltpu.VMEM((rows, d), BF16)],
        compiler_params=_params(1),
        name="ada_mod",
    )(c_all, w, b)


class _Rows:
    def __init__(self, n_rows, tm, tiles_per_group, mod, mod_rows, first_group=0):
        self.n_rows, self.tm, self.tpg, self.mod = n_rows, tm, tiles_per_group, mod
        self.mod_rows, self.first_group = mod_rows, first_group
        self.grid = (n_rows // tm,)

    def row(self, width, col=0):
        return pl.BlockSpec((self.tm, width), lambda i: (i, col))

    def vec(self, width):
        return pl.BlockSpec((1, width), lambda i: (0, 0))

    def modv(self, chunk, width):
        tpg, g0 = self.tpg, self.first_group
        return pl.BlockSpec((None, self.mod_rows, width), lambda i: (g0 + i // tpg, 0, chunk))


def _prenorm_kernel(x_ref, g_ref, sh_ref, sc_ref, h_ref):
    y = _rms(x_ref[...]) * g_ref[...]
    h_ref[...] = (y * (1.0 + sc_ref[...]) + sh_ref[...]).astype(h_ref.dtype)


def _prenorm(rows, x, g, sh_chunk, sc_chunk):
    d = x.shape[1]
    return pl.pallas_call(
        _prenorm_kernel,
        grid=rows.grid,
        in_specs=[rows.row(d), rows.vec(d), rows.modv(sh_chunk, d), rows.modv(sc_chunk, d)],
        out_specs=rows.row(d),
        out_shape=jax.ShapeDtypeStruct(x.shape, BF16),
        compiler_params=_params(1),
        name="prenorm",
    )(x, g, rows.mod, rows.mod)


def _merge_norm_kernel(att_ref, short_ref, ga_ref, gc_ref, o_ref):
    da = att_ref.shape[1]
    o_ref[:, :da] = (_rms(att_ref[...]) * ga_ref[...]).astype(o_ref.dtype)
    o_ref[:, da:] = (_rms(short_ref[...]) * gc_ref[...]).astype(o_ref.dtype)


def _merge_norm(rows, att, short, ga, gc):
    da, dc = att.shape[1], short.shape[1]
    return pl.pallas_call(
        _merge_norm_kernel,
        grid=rows.grid,
        in_specs=[rows.row(da), rows.row(dc), rows.vec(da), rows.vec(dc)],
        out_specs=rows.row(da + dc),
        out_shape=jax.ShapeDtypeStruct((att.shape[0], da + dc), BF16),
        compiler_params=_params(1),
        name="merge_norm",
    )(att, short, ga, gc)


def _resid_prenorm_kernel(x_ref, o_ref, gpost_ref, gt_ref, gpre_ref, sh_ref, sc_ref, x1_ref, h_ref):
    x1 = x_ref[...] + gt_ref[...] * (_rms(o_ref[...]) * gpost_ref[...])
    x1_ref[...] = x1
    y = _rms(x1) * gpre_ref[...]
    h_ref[...] = (y * (1.0 + sc_ref[...]) + sh_ref[...]).astype(h_ref.dtype)


def _resid_prenorm(rows, x, o, gpost, gt_chunk, gpre, sh_chunk, sc_chunk):
    d = x.shape[1]
    return pl.pallas_call(
        _resid_prenorm_kernel,
        grid=rows.grid,
        in_specs=[rows.row(d), rows.row(d), rows.vec(d), rows.modv(gt_chunk, d), rows.vec(d),
                  rows.modv(sh_chunk, d), rows.modv(sc_chunk, d)],
        out_specs=[rows.row(d), rows.row(d)],
        out_shape=[jax.ShapeDtypeStruct(x.shape, F32), jax.ShapeDtypeStruct(x.shape, BF16)],
        compiler_params=_params(1),
        name="resid_prenorm",
    )(x, o, gpost, rows.mod, gpre, rows.mod, rows.mod)


def _resid_kernel(x_ref, f_ref, gpost_ref, gt_ref, y_ref):
    y_ref[...] = x_ref[...] + gt_ref[...] * (_rms(f_ref[...]) * gpost_ref[...])


def _resid(rows, x, f, gpost, gt_chunk):
    d = x.shape[1]
    return pl.pallas_call(
        _resid_kernel,
        grid=rows.grid,
        in_specs=[rows.row(d), rows.row(d), rows.vec(d), rows.modv(gt_chunk, d)],
        out_specs=rows.row(d),
        out_shape=jax.ShapeDtypeStruct(x.shape, F32),
        compiler_params=_params(1),
        name="resid",
    )(x, f, gpost, rows.mod)


def _mm_kernel(a_ref, w_ref, *o_refs, nk, scale):
    acc = jnp.dot(a_ref[...], w_ref[...].astype(BF16), preferred_element_type=F32)
    if nk == 1:
        if scale is not None:
            acc = acc * scale
        for o_ref in o_refs:
            o_ref[...] = acc.astype(o_ref.dtype)
    else:
        o_ref, = o_refs
        k = pl.program_id(2)

        @pl.when(k == 0)
        def _():
            o_ref[...] = acc

        @pl.when(k != 0)
        def _():
            o_ref[...] += acc


def _matmul(a, w, *, col0=0, ncols=None, out_dtypes=(F32,), scale=None, tm_pref=1024, tn_pref=512, tk_pref=5632,
            name="matmul"):
    m, kdim = a.shape
    ncols = w.shape[1] if ncols is None else ncols
    tm, tn, tk = _tile(m, tm_pref), _tile(ncols, tn_pref), _tile(kdim, tk_pref)
    assert col0 % tn == 0
    cb, nk = col0 // tn, kdim // tk
    assert nk == 1 or (tuple(out_dtypes) == (F32,) and scale is None)
    outs = pl.pallas_call(
        functools.partial(_mm_kernel, nk=nk, scale=scale),
        grid=(m // tm, ncols // tn, nk),
        in_specs=[
            pl.BlockSpec((tm, tk), lambda i, j, k: (i, k)),
            pl.BlockSpec((tk, tn), lambda i, j, k: (k, j + cb)),
        ],
        out_specs=[pl.BlockSpec((tm, tn), lambda i, j, k: (i, j)) for _ in out_dtypes],
        out_shape=[jax.ShapeDtypeStruct((m, ncols), dt) for dt in out_dtypes],
        compiler_params=_params(3),
        name=name,
    )(a, w)
    return outs[0] if len(out_dtypes) == 1 else outs


def _proj_kernel(a_ref, w_ref, q_ref, k_ref, k16_ref, v_ref, v16_ref, g_ref, *, nq, q_scale):
    j = pl.program_id(1)

    def tile():
        return jnp.dot(a_ref[...], w_ref[...].astype(BF16), preferred_element_type=F32)

    @pl.when(j < nq)
    def _():
        acc = tile()
        q_ref[...] = (acc if q_scale is None else acc * q_scale).astype(q_ref.dtype)

    @pl.when(jnp.logical_and(j >= nq, j < 2 * nq))
    def _():
        acc = tile()
        k_ref[...] = acc
        k16_ref[...] = acc.astype(k16_ref.dtype)

    @pl.when(jnp.logical_and(j >= 2 * nq, j < 3 * nq))
    def _():
        acc = tile()
        v_ref[...] = acc
        v16_ref[...] = acc.astype(v16_ref.dtype)

    @pl.when(j >= 3 * nq)
    def _():
        g_ref[...] = tile()


def _proj_all(h, w_in, da, dc, q_scale):
    m, d = h.shape
    tm, tn = _tile(m, 2048), _tile(da, 256)
    assert da % tn == 0 and (3 * dc) % tn == 0
    nq, ng = da // tn, 3 * dc // tn

    def seg(first, count):
        return pl.BlockSpec((tm, tn), lambda i, j: (i, jnp.clip(j - first, 0, count - 1)))

    return pl.pallas_call(
        functools.partial(_proj_kernel, nq=nq, q_scale=q_scale),
        grid=(m // tm, 3 * nq + ng),
        in_specs=[
            pl.BlockSpec((tm, d), lambda i, j: (i, 0), pipeline_mode=pl.Buffered(1)),
            pl.BlockSpec((d, tn), lambda i, j: (0, j)),
        ],
        out_specs=[seg(0, nq), seg(nq, nq), seg(nq, nq), seg(2 * nq, nq), seg(2 * nq, nq), seg(3 * nq, ng)],
        out_shape=[jax.ShapeDtypeStruct((m, da), F32 if q_scale is None else BF16), jax.ShapeDtypeStruct((m, da), F32),
                   jax.ShapeDtypeStruct((m, da), BF16), jax.ShapeDtypeStruct((m, da), F32),
                   jax.ShapeDtypeStruct((m, da), BF16), jax.ShapeDtypeStruct((m, 3 * dc), F32)],
        compiler_params=_params(2),
        name="proj_all",
    )(h, w_in)


LOG2E = 1.4426950408889634


def _stick_chains(chains, u2):
    items = [(ci, bi) for ci, ch in enumerate(chains) for bi in range(len(ch[1]))]
    y, sp, drop, cum, a = {}, {}, {}, {}, {}
    for ci, bi in items:
        q, ks, _, bias, _, _, _ = chains[ci]
        y[ci, bi] = lax.dot_general(q, ks[bi], NT_DIMS, preferred_element_type=F32) + bias
    for it in items:
        sp[it] = jnp.maximum(y[it], 0.0) + jnp.log2(1.0 + jnp.exp2(-jnp.abs(y[it])))
        valid = chains[it[0]][6][it[1]]
        drop[it] = sp[it] if valid is None else jnp.where(valid, sp[it], 0.0)
    for it in items:
        hi = drop[it].astype(BF16)
        lo = (drop[it] - hi.astype(F32)).astype(BF16)
        cum[it] = jnp.dot(jnp.concatenate([hi, lo], axis=1), u2, preferred_element_type=F32)
    ds = [ch[4] for ch in chains]
    for it in items:
        ci = it[0]
        valid = chains[ci][6][it[1]]
        e = jnp.exp2((y[it] - sp[it]) - (cum[it] + ds[ci]))
        a[it] = (e if valid is None else jnp.where(valid, e, 0.0)).astype(BF16)
        ds[ci] = ds[ci] + jnp.sum(drop[it], axis=-1, keepdims=True)
    accs = [ch[5] for ch in chains]
    for ci, bi in items:
        accs[ci] = accs[ci] + jnp.dot(a[ci, bi], chains[ci][2][bi], preferred_element_type=F32)
    return list(zip(ds, accs))


def _strict_lower_twice(n):
    r = lax.broadcasted_iota(jnp.int32, (n, n), 0)
    c = lax.broadcasted_iota(jnp.int32, (n, n), 1)
    lower = (r > c).astype(BF16)
    return jnp.concatenate([lower, lower], axis=0)


PATTN_HEADS_PER_STEP = 8


def _pattn_kernel(bias_ref, q_ref, k_ref, v_ref, u_ref, o_ref, *, blk, dh, hps):
    hg, qi = pl.program_id(1), pl.program_id(2)
    u2 = u_ref[...]
    heads = [slice(i * dh, (i + 1) * dh) for i in range(hps)]
    qs = [q_ref[:, hd] for hd in heads]
    biases = [LOG2E * bias_ref[hg * hps + i] for i in range(hps)]

    def visit(kbs, state, valids):
        starts = [pl.multiple_of(kb * blk, blk) for kb in kbs]
        chains = [(qs[i], [k_ref[pl.ds(st, blk), heads[i]] for st in starts],
                   [v_ref[pl.ds(st, blk), heads[i]] for st in starts],
                   biases[i], state[i][0], state[i][1], valids) for i in range(hps)]
        return _stick_chains(chains, u2)

    row = lax.broadcasted_iota(jnp.int32, (blk, blk), 0)
    col = lax.broadcasted_iota(jnp.int32, (blk, blk), 1)
    diag = col < row
    state = [(jnp.zeros((blk, 1), F32), jnp.zeros((blk, dh), F32)) for _ in range(hps)]
    odd = qi & 1
    state = lax.cond(odd == 1, lambda: visit([qi, qi - 1], state, [diag, None]), lambda: visit([qi], state, [diag]))
    nxt = qi - 1 - odd
    state = lax.fori_loop(0, lax.shift_right_logical(qi, 1),
                          lambda it, st: visit([nxt - 2 * it, nxt - 2 * it - 1], st, [None, None]), state)
    for i in range(hps):
        o_ref[:, heads[i]] = state[i][1]


def _prompt_attention(q, k, v, bias, nh, dh):
    b, t, da = q.shape
    blk = _tile(t, 256)
    hps = min(PATTN_HEADS_PER_STEP, nh)
    assert nh % hps == 0
    u2 = _strict_lower_twice(blk)
    qspec = pl.BlockSpec((None, blk, hps * dh), lambda n, h, i: (n, i, h))
    kvspec = pl.BlockSpec((None, t, hps * dh), lambda n, h, i: (n, 0, h))
    return pl.pallas_call(
        functools.partial(_pattn_kernel, blk=blk, dh=dh, hps=hps),
        grid=(b, nh // hps, t // blk),
        in_specs=[
            pl.BlockSpec(memory_space=pltpu.SMEM),
            qspec, kvspec, kvspec,
            pl.BlockSpec((2 * blk, blk), lambda n, h, i: (0, 0)),
        ],
        out_specs=qspec,
        out_shape=jax.ShapeDtypeStruct((b, t, da), F32),
        compiler_params=_params(3),
        name="prompt_attn",
    )(bias, q, k, v, u2)


SATTN_PAGES_PER_STEP = 8
SATTN_HEADS_PER_BLOCK = 16


def _sattn_kernel(pt_ref, q_ref, kn_ref, vn_ref, *rest, nh, dh, nt, page, ppb, hpb, scale):
    nhb = nh // hpb
    n_ops = ppb * nhb
    k_refs, v_refs = rest[:n_ops], rest[n_ops:2 * n_ops]
    bias_ref, u_ref, o_ref, qbd_ref, d_ref, acc_ref, kpad_ref, vpad_ref, kbuf_ref, vbuf_ref = rest[2 * n_ops:]
    p = pl.program_id(1)
    da = nh * dh
    nrows = nt * nh
    log_nh, log_dh = nh.bit_length() - 1, dh.bit_length() - 1
    r = lax.broadcasted_iota(jnp.int32, (nrows, da), 0)
    lane = lax.broadcasted_iota(jnp.int32, (nrows, da), 1)
    own_head = (lane >> log_dh) == (r & (nh - 1))

    @pl.when(p == 0)
    def _():
        qrep = jnp.concatenate([jnp.broadcast_to(q_ref[t:t + 1, :], (nh, da)) for t in range(nt)], axis=0)
        qbd_ref[...] = jnp.where(own_head, qrep * (LOG2E * scale), 0.0).astype(BF16)
        kpad_ref[...] = jnp.zeros(kpad_ref.shape, F32)
        vpad_ref[...] = jnp.zeros(vpad_ref.shape, F32)
        kpad_ref[0:nt, :] = kn_ref[...]
        vpad_ref[0:nt, :] = vn_ref[...]
        rr = lax.broadcasted_iota(jnp.int32, (nrows, page), 0)
        cc = lax.broadcasted_iota(jnp.int32, (nrows, page), 1)
        valid = cc < (rr >> log_nh)
        u2_new = jnp.concatenate([u_ref[0:page, 0:page], u_ref[2 * page:3 * page, 0:page]], axis=0)
        (d, acc), = _stick_chains([(qbd_ref[...], [kpad_ref[...].astype(BF16)], [vpad_ref[...].astype(BF16)],
                                    bias_ref[:, 0:page], jnp.zeros((nrows, 1), F32), jnp.zeros((nrows, da), F32),
                                    [valid])], u2_new)
        d_ref[...] = d
        acc_ref[...] = acc

    def regroup(src_refs, dst_ref):
        for j in range(ppb):
            for hb in range(nhb):
                by_head = jnp.swapaxes(src_refs[j * nhb + hb][...].astype(BF16), 0, 1)
                for s in range(hpb):
                    head = hb * hpb + s
                    dst_ref[j * page:(j + 1) * page, head * dh:(head + 1) * dh] = by_head[s]

    regroup(k_refs, kbuf_ref)
    regroup(v_refs, vbuf_ref)

    chunk = 2 * page
    order = list(reversed(range(ppb * page // chunk)))
    (d, acc), = _stick_chains([(qbd_ref[...], [kbuf_ref[ci * chunk:(ci + 1) * chunk, :] for ci in order],
                                [vbuf_ref[ci * chunk:(ci + 1) * chunk, :] for ci in order],
                                bias_ref[...], d_ref[...], acc_ref[...], [None] * len(order))], u_ref[...])
    d_ref[...] = d
    acc_ref[...] = acc

    @pl.when(p == pl.num_programs(1) - 1)
    def _():
        own = jnp.where(own_head, acc, 0.0)
        o_ref[...] = jnp.sum(own.reshape(nt, nh, da), axis=1)


def _sample_attention(q, k_new, v_new, k_pages, v_pages, page_table, bias, nh, dh):
    ns, nt, da = q.shape
    page = k_pages.shape[1]
    npg = page_table.shape[1]
    ppb = min(SATTN_PAGES_PER_STEP, npg)
    hpb = min(SATTN_HEADS_PER_BLOCK, nh)
    assert npg % ppb == 0 and ppb % 2 == 0 and nh % hpb == 0
    assert nh & (nh - 1) == 0 and dh & (dh - 1) == 0 and nt <= page
    nrows = nt * nh
    u2 = _strict_lower_twice(2 * page)
    bias_rows = jnp.broadcast_to(jnp.tile(LOG2E * bias.astype(F32), nt)[:, None], (nrows, 2 * page))
    pt = page_table.reshape(-1).astype(jnp.int32)

    seq = pl.BlockSpec((None, nt, da), lambda n, p, pt: (n, 0, 0))

    def page_spec(j, hb):
        return pl.BlockSpec((None, page, hpb, dh),
                            lambda n, p, pt: (pt[n * npg + npg - ppb * (p + 1) + j], 0, hb, 0))

    page_specs = [page_spec(j, hb) for j in range(ppb) for hb in range(nh // hpb)]
    const = lambda shape: pl.BlockSpec(shape, lambda n, p, pt: (0, 0))
    grid_spec = pltpu.PrefetchScalarGridSpec(
        num_scalar_prefetch=1,
        grid=(ns, npg // ppb),
        in_specs=[seq, seq, seq] + page_specs + page_specs + [const((nrows, 2 * page)), const((4 * page, 2 * page))],
        out_specs=seq,
        scratch_shapes=[
            pltpu.VMEM((nrows, da), BF16),
            pltpu.VMEM((nrows, 1), F32),
            pltpu.VMEM((nrows, da), F32),
            pltpu.VMEM((page, da), F32),
            pltpu.VMEM((page, da), F32),
            pltpu.VMEM((ppb * page, da), BF16),
            pltpu.VMEM((ppb * page, da), BF16),
        ],
    )
    n_ops = len(page_specs)
    return pl.pallas_call(
        functools.partial(_sattn_kernel, nh=nh, dh=dh, nt=nt, page=page, ppb=ppb, hpb=hpb, scale=dh ** -0.5),
        grid_spec=grid_spec,
        out_shape=jax.ShapeDtypeStruct((ns, nt, da), F32),
        compiler_params=_params(2),
        name="sample_attn",
    )(pt, q, k_new, v_new, *([k_pages] * n_ops), *([v_pages] * n_ops), bias_rows, u2)


def _shift_rows(x, s):
    row = lax.broadcasted_iota(jnp.int32, x.shape, 0)
    return jnp.where(row >= s, pltpu.roll(x, s, 0), 0.0)


def _conv_rows(u, w_ref):
    return w_ref[0:1, :] * _shift_rows(u, 2) + w_ref[1:2, :] * _shift_rows(u, 1) + w_ref[2:3, :] * u


def _conv_time_major(us, prev_ref, w_ref):
    seq = [prev_ref[0], prev_ref[1]] + us
    ys = [w_ref[0:1, :] * seq[t] + w_ref[1:2, :] * seq[t + 1] + w_ref[2:3, :] * seq[t + 2] for t in range(len(us))]
    return ys, seq[-2:]


def _convgate_p_kernel(b_ref, c_ref, u_ref, w_ref, short_ref, st_ref):
    t = c_ref.shape[0]
    st_ref[...] = c_ref[t - 2:t, :] * u_ref[t - 2:t, :]
    short_ref[...] = b_ref[...] * _conv_rows(c_ref[...] * u_ref[...], w_ref)


def _convgate_prompt(pg, conv_w, n_seq, t, dc):
    tc = _tile(dc, 256)
    nj = dc // tc
    return pl.pallas_call(
        _convgate_p_kernel,
        grid=(n_seq, nj),
        in_specs=[
            pl.BlockSpec((t, tc), lambda n, j: (n, j)),
            pl.BlockSpec((t, tc), lambda n, j: (n, j + nj)),
            pl.BlockSpec((t, tc), lambda n, j: (n, j + 2 * nj)),
            pl.BlockSpec((CONV_WIDTH, tc), lambda n, j: (0, j)),
        ],
        out_specs=[pl.BlockSpec((t, tc), lambda n, j: (n, j)), pl.BlockSpec((None, 2, tc), lambda n, j: (n, 0, j))],
        out_shape=[jax.ShapeDtypeStruct((n_seq * t, dc), F32), jax.ShapeDtypeStruct((n_seq, 2, dc), F32)],
        compiler_params=_params(2),
        name="convgate_prompt",
    )(pg, pg, pg, conv_w)


FC1_ROW_CHUNKS = 16


def _conv_rows_after(u, tail, w_ref):
    row = lax.broadcasted_iota(jnp.int32, u.shape, 0)
    p1, p2 = tail[7:8, :], tail[6:7, :]
    s1 = jnp.where(row == 0, p1, pltpu.roll(u, 1, 0))
    s2 = jnp.where(row == 0, p2, jnp.where(row == 1, p1, pltpu.roll(u, 2, 0)))
    return w_ref[0:1, :] * s2 + w_ref[1:2, :] * s1 + w_ref[2:3, :] * u


def _fc1_gate_kernel(h_ref, wa_ref, wb_ref, cw_ref, *rest, prev_steps):
    wa = wa_ref[...].astype(BF16)
    wb = wb_ref[...].astype(BF16)
    if prev_steps == 0:
        g_ref, st_ref = rest
        t = h_ref.shape[0]
        n_chunks = FC1_ROW_CHUNKS if t % (8 * FC1_ROW_CHUNKS) == 0 else 1
        rc = t // n_chunks

        def dots(r):
            h = h_ref[r * rc:(r + 1) * rc, :]
            return jnp.dot(h, wa, preferred_element_type=F32), jnp.dot(h, wb, preferred_element_type=F32)

        def gate(r, ab, tail):
            a, b = ab
            g_ref[r * rc:(r + 1) * rc, :] = (_silu(_conv_rows_after(a, tail, cw_ref)) * b).astype(g_ref.dtype)

        tail = jnp.zeros((8, wa.shape[1]), F32)
        pending = dots(0)
        for r in range(1, n_chunks):
            nxt = dots(r)
            gate(r - 1, pending, tail)
            tail = pending[0][rc - 8:rc, :]
            pending = nxt
        gate(n_chunks - 1, pending, tail)
        st_ref[...] = pending[0][rc - 8:rc, :]
    else:
        prev_ref, g_ref, st_ref = rest
        h = h_ref[...]
        a = jnp.dot(h, wa, preferred_element_type=F32)
        b = jnp.dot(h, wb, preferred_element_type=F32)
        ns = prev_ref.shape[1]
        ys, last = _conv_time_major([a[t * ns:(t + 1) * ns, :] for t in range(prev_steps)], prev_ref, cw_ref)
        for t in range(prev_steps):
            g_ref[t * ns:(t + 1) * ns, :] = (_silu(ys[t]) * b[t * ns:(t + 1) * ns, :]).astype(g_ref.dtype)
        st_ref[0] = last[0]
        st_ref[1] = last[1]


def _fc1_gate(h, w_fc1, conv_w, dff, *, tm, prev=None):
    m, d = h.shape
    tn = _tile(dff, 256)
    nj = dff // tn
    in_specs = [
        pl.BlockSpec((tm, d), lambda i, j: (i, 0), pipeline_mode=pl.Buffered(1)),
        pl.BlockSpec((d, tn), lambda i, j: (0, j)),
        pl.BlockSpec((d, tn), lambda i, j: (0, j + nj)),
        pl.BlockSpec((CONV_WIDTH, tn), lambda i, j: (0, j)),
    ]
    args = [h, w_fc1, w_fc1, conv_w]
    if prev is None:
        prev_steps = 0
        st_spec = pl.BlockSpec((None, 8, tn), lambda i, j: (i, 0, j))
        st_shape = jax.ShapeDtypeStruct((m // tm, 8, dff), F32)
    else:
        ns = prev.shape[1]
        prev_steps = m // ns
        assert tm == m
        in_specs.append(pl.BlockSpec((2, ns, tn), lambda i, j: (0, 0, j)))
        args.append(prev)
        st_spec = pl.BlockSpec((2, ns, tn), lambda i, j: (0, 0, j))
        st_shape = jax.ShapeDtypeStruct((2, ns, dff), F32)
    return pl.pallas_call(
        functools.partial(_fc1_gate_kernel, prev_steps=prev_steps),
        grid=(m // tm, nj),
        in_specs=in_specs,
        out_specs=[pl.BlockSpec((tm, tn), lambda i, j: (i, j)), st_spec],
        out_shape=[jax.ShapeDtypeStruct((m, dff), BF16), st_shape],
        compiler_params=_params(2),
        name="fc1_gate",
    )(*args)


def _convgate_s_kernel(b_ref, c_ref, u_ref, prev_ref, w_ref, short_ref, st_ref, *, nt):
    ns = prev_ref.shape[1]
    cu = [c_ref[t * ns:(t + 1) * ns, :] * u_ref[t * ns:(t + 1) * ns, :] for t in range(nt)]
    ys, last = _conv_time_major(cu, prev_ref, w_ref)
    for t in range(nt):
        short_ref[t * ns:(t + 1) * ns, :] = b_ref[t * ns:(t + 1) * ns, :] * ys[t]
    st_ref[0] = last[0]
    st_ref[1] = last[1]


def _convgate_sample(pg, prev, conv_w, nt, ns, dc):
    tc = _tile(dc, 512)
    nj = dc // tc
    rows = nt * ns
    return pl.pallas_call(
        functools.partial(_convgate_s_kernel, nt=nt),
        grid=(nj,),
        in_specs=[
            pl.BlockSpec((rows, tc), lambda j: (0, j)),
            pl.BlockSpec((rows, tc), lambda j: (0, j + nj)),
            pl.BlockSpec((rows, tc), lambda j: (0, j + 2 * nj)),
            pl.BlockSpec((2, ns, tc), lambda j: (0, 0, j)),
            pl.BlockSpec((CONV_WIDTH, tc), lambda j: (0, j)),
        ],
        out_specs=[pl.BlockSpec((rows, tc), lambda j: (0, j)), pl.BlockSpec((2, ns, tc), lambda j: (0, 0, j))],
        out_shape=[jax.ShapeDtypeStruct((rows, dc), F32), jax.ShapeDtypeStruct((2, ns, dc), F32)],
        compiler_params=_params(1),
        name="convgate_sample",
    )(pg, pg, pg, prev, conv_w)


MOD_SH1, MOD_SC1, MOD_GT1, MOD_SH2, MOD_SC2, MOD_GT2 = range(6)


def _layer(x, rows, attend, convgate, ffn_gate, wts, da, dc, dff, q_scale=None):
    h = _prenorm(rows, x, wts["g_pre_mix"], MOD_SH1, MOD_SC1)
    q, k, k_att, v, v_att, pg = _proj_all(h, wts["w_in"], da, dc, q_scale)
    if q_scale is None:
        k_att, v_att = k, v
    att = attend(q, k_att, v_att)
    short, conv_state = convgate(pg)
    merged = _merge_norm(rows, att, short, wts["g_attn_out"], wts["g_conv_out"])
    o = _matmul(merged, wts["w_o"], name="out_proj")
    x1, h2 = _resid_prenorm(rows, x, o, wts["g_post_mix"], MOD_GT1, wts["g_pre_ffn"], MOD_SH2, MOD_SC2)
    g, ffn_state = ffn_gate(h2)
    f = _matmul(g, wts["w_fc2"], tm_pref=512, tk_pref=g.shape[1], name="fc2")
    y = _resid(rows, x1, f, wts["g_post_ffn"], MOD_GT2)
    return y, k, v, conv_state, ffn_state


def kernel(x_prompt, x_sample, c_prompt, c_sample, cache_k, cache_v, state_conv, state_ffn_conv, page_table, w_ada, b_ada, g_pre_mix, g_post_mix, w_in, sb_bias, conv_w, g_attn_out, g_conv_out, w_o, g_pre_ffn, g_post_ffn, w_fc1, ffn_conv_w, w_fc2):
    b, t, d = x_prompt.shape
    ns, nt, _ = x_sample.shape
    depth, n_phys, page, nh, dh = cache_k.shape
    assert depth == 1 and conv_w.shape[1] == CONV_WIDTH and ffn_conv_w.shape[1] == CONV_WIDTH
    da, dc, dff = nh * dh, conv_w.shape[-1], ffn_conv_w.shape[-1]

    wts = {
        "w_in": w_in.reshape(d, -1),
        "w_o": w_o.reshape(da + dc, d),
        "w_fc1": w_fc1.reshape(d, 2 * dff),
        "w_fc2": w_fc2.reshape(dff, d).astype(BF16),
        "g_pre_mix": g_pre_mix.reshape(1, d), "g_post_mix": g_post_mix.reshape(1, d),
        "g_attn_out": g_attn_out.reshape(1, da), "g_conv_out": g_conv_out.reshape(1, dc),
        "g_pre_ffn": g_pre_ffn.reshape(1, d), "g_post_ffn": g_post_ffn.reshape(1, d),
    }
    cw, fcw = conv_w.reshape(CONV_WIDTH, dc), ffn_conv_w.reshape(CONV_WIDTH, dff)
    bias = sb_bias.reshape(nh)

    n_c = ns + b
    pad = -n_c % 16
    c_all = jnp.concatenate([c_sample, c_prompt, jnp.zeros((pad, d), F32)], axis=0)
    mod = _ada(c_all, w_ada.reshape(d, 6 * d), b_ada.reshape(1, 6 * d))
    assert ns % 8 == 0
    mod_s = mod.reshape(1, n_c + pad, 6 * d)
    mod_p = mod.reshape(n_c + pad, 1, 6 * d)

    tm_p = _tile(t, 256)
    rows_p = _Rows(b * t, tm_p, t // tm_p, mod_p, 1, first_group=ns)

    def attend_p(q, k, v):
        r3 = lambda a: a.reshape(b, t, da)
        return _prompt_attention(r3(q), r3(k), r3(v), bias, nh, dh).reshape(b * t, da)

    yp, kp, vp, cp, fp = _layer(
        x_prompt.reshape(b * t, d), rows_p, attend_p,
        lambda pg: _convgate_prompt(pg, cw, b, t, dc),
        lambda h2: _fc1_gate(h2, wts["w_fc1"], fcw, dff, tm=t),
        wts, da, dc, dff, q_scale=LOG2E * dh ** -0.5)

    rows_s = _Rows(nt * ns, ns, nt, mod_s, ns)
    to_seq_major = lambda a: a.reshape(nt, ns, -1).transpose(1, 0, 2)
    to_time_major = lambda a: a.transpose(1, 0, 2).reshape(nt * ns, -1)
    kv_seq = {}

    def attend_s(q, k, v):
        kv_seq["k"], kv_seq["v"] = to_seq_major(k), to_seq_major(v)
        att = _sample_attention(to_seq_major(q), kv_seq["k"], kv_seq["v"],
                                cache_k.reshape(n_phys, page, nh, dh), cache_v.reshape(n_phys, page, nh, dh),
                                page_table, bias, nh, dh)
        return to_time_major(att)

    prev_conv = state_conv.reshape(ns, CONV_WIDTH - 1, dc).transpose(1, 0, 2)
    prev_ffn = state_ffn_conv.reshape(ns, CONV_WIDTH - 1, dff).transpose(1, 0, 2)
    ys, _, _, cs, fs = _layer(
        to_time_major(x_sample), rows_s, attend_s,
        lambda pg: _convgate_sample(pg, prev_conv, cw, nt, ns, dc),
        lambda h2: _fc1_gate(h2, wts["w_fc1"], fcw, dff, tm=nt * ns, prev=prev_ffn),
        wts, da, dc, dff)

    return (
        yp.reshape(b, t, d),
        to_seq_major(ys),
        kp.reshape(1, b, t, nh, dh),
        vp.reshape(1, b, t, nh, dh),
        cp.reshape(1, b, CONV_WIDTH - 1, dc),
        fp[:, 8 - (CONV_WIDTH - 1):, :].reshape(1, b, CONV_WIDTH - 1, dff),
        kv_seq["k"].reshape(1, ns, nt, nh, dh),
        kv_seq["v"].reshape(1, ns, nt, nh, dh),
        cs.transpose(1, 0, 2).reshape(1, ns, CONV_WIDTH - 1, dc),
        fs.transpose(1, 0, 2).reshape(1, ns, CONV_WIDTH - 1, dff),
    )
```
